```python
import jax, jax.numpy as jnp
from jax import lax
import numpy as np

D_MODEL = 1024
BATCH = 8
SEQ = 2048
DEPTH = 1
DEC_BATCH = 32
DEC_SEQ = 8
PAST_LEN = 16384
PAGE_SIZE = 128

HEAD_DIM = 64
H_A = D_MODEL // (2 * HEAD_DIM)
H_B = D_MODEL // (2 * HEAD_DIM)
W_A = H_A * HEAD_DIM
W_B = H_B * HEAD_DIM
H_IDX = 8
D_IDX = 64
TOPK_MAX = 256
ROPE_THETA = 10000.0
FORGET_BIAS = 2.0
_FF_RAW = -(-8 * D_MODEL // 3)
D_FF = -(-_FF_RAW // 256) * 256
Q_BLOCK = 128
EPS = 1e-6
ATTN_SCALE = HEAD_DIM ** -0.5
IDX_SCALE = (H_IDX * D_IDX) ** -0.5
SPLITS = (W_A, W_A, W_A, H_IDX * D_IDX, D_IDX, H_IDX, W_B, W_B, W_B, H_B, D_MODEL, D_MODEL)

kernel_name = "hybrid_dsa_fox_gated_decoder_step"


def _rmsnorm(x, g):
    xf = x.astype(jnp.float32)
    y = xf * lax.rsqrt(jnp.mean(xf * xf, axis=-1, keepdims=True) + EPS)
    return (y * g.astype(jnp.float32)).astype(x.dtype)


def _rope(x, pos):
    half = x.shape[-1] // 2
    inv = jnp.power(jnp.float32(ROPE_THETA), -jnp.arange(half, dtype=jnp.float32) / half)
    ang = pos.astype(jnp.float32)[:, None] * inv[None, :]
    cos = jnp.cos(ang)[None, :, None, :]
    sin = jnp.sin(ang)[None, :, None, :]
    x1 = x[..., :half].astype(jnp.float32)
    x2 = x[..., half:].astype(jnp.float32)
    return jnp.concatenate([x1 * cos - x2 * sin, x2 * cos + x1 * sin], axis=-1).astype(x.dtype)


def _project(xn, w_in, b_f, qn_a, kn_a, qn_b, kn_b, pos):
    B, T, _ = xn.shape
    h = xn @ w_in
    cuts = [int(c) for c in np.cumsum(SPLITS)[:-1]]
    q_a, k_a, v_a, q_i, k_i, w_i, q_b, k_b, v_b, f_b, g_a, g_b = jnp.split(h, cuts, axis=-1)
    q_a = _rope(_rmsnorm(q_a.reshape(B, T, H_A, HEAD_DIM), qn_a), pos)
    k_a = _rope(_rmsnorm(k_a.reshape(B, T, H_A, HEAD_DIM), kn_a), pos)
    v_a = v_a.reshape(B, T, H_A, HEAD_DIM)
    q_i = _rope(q_i.reshape(B, T, H_IDX, D_IDX), pos)
    k_i = _rope(k_i[:, :, None, :], pos)[:, :, 0, :]
    q_b = _rmsnorm(q_b.reshape(B, T, H_B, HEAD_DIM), qn_b)
    k_b = _rmsnorm(k_b.reshape(B, T, H_B, HEAD_DIM), kn_b)
    v_b = v_b.reshape(B, T, H_B, HEAD_DIM)
    logf = jax.nn.log_sigmoid((f_b + b_f).astype(jnp.float32))
    return q_a, k_a, v_a, q_i, k_i, w_i, q_b, k_b, v_b, logf, g_a, g_b


def _index_scores(q_i, w_i, k_i):
    dots = jnp.einsum('bthd,bsd->bths', q_i, k_i, preferred_element_type=jnp.float32)
    return jnp.einsum('bths,bth->bts', jax.nn.relu(dots), w_i.astype(jnp.float32)) * IDX_SCALE


def _sparse_attend(q, k_sel, v_sel, valid):
    s = jnp.einsum('bthd,btkhd->bthk', q, k_sel, preferred_element_type=jnp.float32) * ATTN_SCALE
    s = jnp.where(valid[:, :, None, :], s, -jnp.inf)
    p = jax.nn.softmax(s, axis=-1)
    return jnp.einsum('bthk,btkhd->bthd', p.astype(v_sel.dtype), v_sel)


def _decay_offsets(logf):
    incl = jnp.cumsum(logf[:, ::-1], axis=1)[:, ::-1]
    return incl - logf


def _mixer_a_prompt(q_a, k_a, v_a, q_i, w_i, k_i):
    B, S = q_a.shape[:2]
    topk = min(TOPK_MAX, S // 4)
    nb = S // Q_BLOCK
    key_pos = jnp.arange(S)
    b_ix = jnp.arange(B)[:, None, None]

    def blk(args):
        qa, qi, wi, qpos = args
        score = _index_scores(qi, wi, k_i)
        score = jnp.where(key_pos[None, None, :] <= qpos[None, :, None], score, -jnp.inf)
        _, idx = lax.top_k(score, topk)
        valid = idx <= qpos[None, :, None]
        return _sparse_attend(qa, k_a[b_ix, idx], v_a[b_ix, idx], valid)

    to_blocks = lambda t: jnp.moveaxis(t.reshape((B, nb, Q_BLOCK) + t.shape[2:]), 1, 0)
    out = lax.map(blk, (to_blocks(q_a), to_blocks(q_i), to_blocks(w_i), key_pos.reshape(nb, Q_BLOCK)))
    return jnp.moveaxis(out, 0, 1).reshape(B, S, H_A, HEAD_DIM)


def _mixer_a_sample(q_a, k_a, v_a, q_i, w_i, k_i, pool_kv, pool_ki, page_table):
    B, T = q_a.shape[:2]
    past = page_table.shape[1] * PAGE_SIZE
    L = past + T
    topk = min(TOPK_MAX, L // 4)
    ki_all = jnp.concatenate([pool_ki[page_table].reshape(B, past, D_IDX), k_i], axis=1)
    qpos = past + jnp.arange(T)
    key_pos = jnp.arange(L)
    score = _index_scores(q_i, w_i, ki_all)
    score = jnp.where(key_pos[None, None, :] <= qpos[None, :, None], score, -jnp.inf)
    _, idx = lax.top_k(score, topk)
    valid = idx <= qpos[None, :, None]
    b_ix = jnp.arange(B)[:, None, None]
    pidx = jnp.minimum(idx, past - 1)
    phys = page_table[b_ix, pidx // PAGE_SIZE]
    kv_past = pool_kv[phys, pidx % PAGE_SIZE]
    kv_new = jnp.stack([k_a, v_a], axis=2)[b_ix, jnp.clip(idx - past, 0, T - 1)]
    kv = jnp.where((idx < past)[..., None, None, None], kv_past, kv_new)
    return _sparse_attend(q_a, kv[:, :, :, 0], kv[:, :, :, 1], valid)


def _mixer_b_prompt(q_b, k_b, v_b, logf):
    B, S = q_b.shape[:2]
    nb = S // Q_BLOCK
    g = _decay_offsets(logf)
    g_keys = jnp.transpose(g, (0, 2, 1))[:, :, None, :]
    key_pos = jnp.arange(S)

    def blk(args):
        q, gq, qpos = args
        s = jnp.einsum('bqhd,bshd->bhqs', q, k_b, preferred_element_type=jnp.float32) * ATTN_SCALE
        s = s + g_keys - jnp.transpose(gq, (0, 2, 1))[..., None]
        s = jnp.where(key_pos[None, None, None, :] <= qpos[None, None, :, None], s, -jnp.inf)
        p = jax.nn.softmax(s, axis=-1)
        return jnp.einsum('bhqs,bshd->bqhd', p.astype(v_b.dtype), v_b)

    to_blocks = lambda t: jnp.moveaxis(t.reshape((B, nb, Q_BLOCK) + t.shape[2:]), 1, 0)
    out = lax.map(blk, (to_blocks(q_b), to_blocks(g), key_pos.reshape(nb, Q_BLOCK)))
    return jnp.moveaxis(out, 0, 1).reshape(B, S, H_B, HEAD_DIM)


def _mixer_b_sample(q_b, k_b, v_b, logf, pool_kv, pool_logf, page_table):
    B, T = q_b.shape[:2]
    past = page_table.shape[1] * PAGE_SIZE
    L = past + T
    k_all = jnp.concatenate([pool_kv[page_table, :, 0].reshape(B, past, H_B, HEAD_DIM), k_b], axis=1)
    v_all = jnp.concatenate([pool_kv[page_table, :, 1].reshape(B, past, H_B, HEAD_DIM), v_b], axis=1)
    logf_all = jnp.concatenate([pool_logf[page_table].reshape(B, past, H_B).astype(jnp.float32), logf], axis=1)
    g = _decay_offsets(logf_all)
    qpos = past + jnp.arange(T)
    key_pos = jnp.arange(L)
    s = jnp.einsum('bthd,bshd->bhts', q_b, k_all, preferred_element_type=jnp.float32) * ATTN_SCALE
    s = s + jnp.transpose(g, (0, 2, 1))[:, :, None, :] - jnp.transpose(g[:, past:], (0, 2, 1))[..., None]
    s = jnp.where(key_pos[None, None, None, :] <= qpos[None, None, :, None], s, -jnp.inf)
    p = jax.nn.softmax(s, axis=-1)
    return jnp.einsum('bhts,bshd->bthd', p.astype(v_all.dtype), v_all)


def _merge_ffn(x, o_a, o_b, g_a, g_b, w_up_a, w_up_b, w_o, norm2, w_ffn_in, w_ffn_out):
    B, T, _ = x.shape
    u_a = o_a.reshape(B, T, W_A) @ w_up_a
    u_b = o_b.reshape(B, T, W_B) @ w_up_b
    m = jax.nn.sigmoid(g_a) * u_a + jax.nn.sigmoid(g_b) * u_b
    x = x + m @ w_o
    hn = _rmsnorm(x, norm2)
    gate, up = jnp.split(hn @ w_ffn_in, 2, axis=-1)
    return x + (jax.nn.silu(gate) * up) @ w_ffn_out


def setup_inputs(seed: int = 0) -> dict:
    key = jax.random.key(seed)
    ks = jax.random.split(key, 24)
    f32 = jnp.float32
    n_pages = PAST_LEN // PAGE_SIZE
    n_used = DEC_BATCH * n_pages
    n_pool = n_used + max(1, n_used // 4)
    d_in = sum(SPLITS)
    nrm = lambda k, s: jax.random.normal(k, s, f32)
    wt = lambda k, s, fan: jax.random.normal(k, s, f32) * fan ** -0.5
    gain = lambda k, s: 1.0 + 0.05 * jax.random.normal(k, s, f32)
    return {
        "x_prompt": nrm(ks[0], (BATCH, SEQ, D_MODEL)),
        "x_sample": nrm(ks[1], (DEC_BATCH, DEC_SEQ, D_MODEL)),
        "cache_a_kv": nrm(ks[2], (DEPTH, n_pool, PAGE_SIZE, 2, H_A, HEAD_DIM)),
        "cache_idx_k": nrm(ks[3], (DEPTH, n_pool, PAGE_SIZE, D_IDX)),
        "cache_b_kv": nrm(ks[4], (DEPTH, n_pool, PAGE_SIZE, 2, H_B, HEAD_DIM)),
        "cache_b_logf": jax.nn.log_sigmoid(FORGET_BIAS + nrm(ks[5], (DEPTH, n_pool, PAGE_SIZE, H_B))),
        "page_table": jax.random.permutation(ks[6], n_pool)[:n_used].reshape(DEC_BATCH, n_pages).astype(jnp.int32),
        "norm1": gain(ks[7], (DEPTH, D_MODEL)),
        "w_in": wt(ks[8], (DEPTH, D_MODEL, d_in), D_MODEL),
        "b_f": FORGET_BIAS + 0.1 * nrm(ks[9], (DEPTH, H_B)),
        "qn_a": gain(ks[10], (DEPTH, HEAD_DIM)),
        "kn_a": gain(ks[11], (DEPTH, HEAD_DIM)),
        "qn_b": gain(ks[12], (DEPTH, HEAD_DIM)),
        "kn_b": gain(ks[13], (DEPTH, HEAD_DIM)),
        "w_up_a": wt(ks[14], (DEPTH, W_A, D_MODEL), W_A),
        "w_up_b": wt(ks[15], (DEPTH, W_B, D_MODEL), W_B),
        "w_o": wt(ks[16], (DEPTH, D_MODEL, D_MODEL), D_MODEL),
        "norm2": gain(ks[17], (DEPTH, D_MODEL)),
        "w_ffn_in": wt(ks[18], (DEPTH, D_MODEL, 2 * D_FF), D_MODEL),
        "w_ffn_out": wt(ks[19], (DEPTH, D_FF, D_MODEL), D_FF),
    }


def reference(x_prompt, x_sample, cache_a_kv, cache_idx_k, cache_b_kv, cache_b_logf, page_table,
              norm1, w_in, b_f, qn_a, kn_a, qn_b, kn_b, w_up_a, w_up_b, w_o, norm2, w_ffn_in, w_ffn_out):
    S = x_prompt.shape[1]
    T = x_sample.shape[1]
    past = page_table.shape[1] * PAGE_SIZE
    pos_p = jnp.arange(S)
    pos_s = past + jnp.arange(T)
    h_p, h_s = x_prompt, x_sample
    pa, pi, pb, pl, sa, si, sb, sl = [], [], [], [], [], [], [], []
    for l in range(DEPTH):
        proj_w = (w_in[l], b_f[l], qn_a[l], kn_a[l], qn_b[l], kn_b[l])
        out_w = (w_up_a[l], w_up_b[l], w_o[l], norm2[l], w_ffn_in[l], w_ffn_out[l])
        q_a, k_a, v_a, q_i, k_i, w_i, q_b, k_b, v_b, logf, g_a, g_b = _project(_rmsnorm(h_p, norm1[l]), *proj_w, pos_p)
        o_a = _mixer_a_prompt(q_a, k_a, v_a, q_i, w_i, k_i)
        o_b = _mixer_b_prompt(q_b, k_b, v_b, logf)
        h_p = _merge_ffn(h_p, o_a, o_b, g_a, g_b, *out_w)
        pa.append(jnp.stack([k_a, v_a], axis=2)); pi.append(k_i)
        pb.append(jnp.stack([k_b, v_b], axis=2)); pl.append(logf.astype(x_prompt.dtype))
        q_a, k_a, v_a, q_i, k_i, w_i, q_b, k_b, v_b, logf, g_a, g_b = _project(_rmsnorm(h_s, norm1[l]), *proj_w, pos_s)
        o_a = _mixer_a_sample(q_a, k_a, v_a, q_i, w_i, k_i, cache_a_kv[l], cache_idx_k[l], page_table)
        o_b = _mixer_b_sample(q_b, k_b, v_b, logf, cache_b_kv[l], cache_b_logf[l], page_table)
        h_s = _merge_ffn(h_s, o_a, o_b, g_a, g_b, *out_w)
        sa.append(jnp.stack([k_a, v_a], axis=2)); si.append(k_i)
        sb.append(jnp.stack([k_b, v_b], axis=2)); sl.append(logf.astype(x_sample.dtype))
    p_a_kv, p_idx_k, p_b_kv, p_b_logf = jnp.stack(pa), jnp.stack(pi), jnp.stack(pb), jnp.stack(pl)
    s_a_kv, s_idx_k, s_b_kv, s_b_logf = jnp.stack(sa), jnp.stack(si), jnp.stack(sb), jnp.stack(sl)
    return (h_p, h_s, p_a_kv, p_idx_k, p_b_kv, p_b_logf, s_a_kv, s_idx_k, s_b_kv, s_b_logf)
```

```python
import functools

import jax
import jax.numpy as jnp
import numpy as np
from jax import lax
from jax.experimental import pallas as pl
from jax.experimental.pallas import tpu as pltpu

HEAD_DIM = 64
N_HEADS = 8
D_IDX = 64
PAGE_SIZE = 128
TOPK_MAX = 256
ROPE_THETA = 10000.0
EPS = 1e-6
ATTN_SCALE = HEAD_DIM ** -0.5
IDX_SCALE = (N_HEADS * D_IDX) ** -0.5
Q_BLOCK = 128
ROW_TILE = 256
LANES = 128
SMALL_W = 128
WI_OFF = D_IDX
F_OFF = D_IDX + N_HEADS
NEG = -1e30
VMEM_LIMIT = 56 * 1024 * 1024
INT_MIN = -2 ** 31

bf16 = jnp.bfloat16
f32 = jnp.float32


def _nt_dot(a, b):
    return lax.dot_general(a, b, (((1,), (1,)), ((), ())), preferred_element_type=f32)


def _dot(a, b):
    return jnp.dot(a, b, preferred_element_type=f32)


def _split3(x):
    a1 = x.astype(bf16)
    r1 = x - a1.astype(f32)
    a2 = r1.astype(bf16)
    a3 = (r1 - a2.astype(f32)).astype(bf16)
    return a1, a2, a3


def _log_sigmoid(x):
    return -(jnp.maximum(-x, 0.0) + jnp.log1p(jnp.exp(-jnp.abs(x))))


def _sigmoid(x):
    return 1.0 / (1.0 + jnp.exp(-x))


def _rmsnorm_rows(x, g):
    return x * lax.rsqrt(jnp.mean(x * x, axis=-1, keepdims=True) + EPS) * g


def _head_rmsnorm(x, g, bd):
    x2 = x * x
    hi = x2.astype(bf16)
    lo = (x2 - hi.astype(f32)).astype(bf16)
    ss = _dot(hi, bd) + _dot(lo, bd)
    return x * lax.rsqrt(ss * (1.0 / HEAD_DIM) + EPS) * g


def _rope(x, cos, sin_signed):
    w = x.shape[-1]
    half = HEAD_DIM // 2
    lane = lax.broadcasted_iota(jnp.int32, x.shape, 1)
    first = (lane & half) == 0
    partner = jnp.where(first, pltpu.roll(x, w - half, 1), pltpu.roll(x, half, 1))
    return x * cos + partner * sin_signed


def _proj_kernel(x_ref, n1_ref, w_ref, wft_ref, bfc_ref, bfr_ref, gqa_ref, gka_ref, gqb_ref, gkb_ref,
                 cos_ref, sin_ref, cos_s_ref, sin_s_ref, bd_ref,
                 akv_ref, idxk_ref, bkv_ref, logf_ref,
                 qa_ref, ka_ref, va_ref, qi_ref, qb_ref, kb_ref, vb_ref,
                 aux_ref, ki_ref, logft_ref, ga_ref, gb_ref):
    W = N_HEADS * HEAD_DIM
    xn = _rmsnorm_rows(x_ref[...], n1_ref[...])
    xb = xn.astype(bf16)
    bd = bd_ref[...]
    cos = cos_ref[...]
    sin = sin_ref[...]

    def heads_out(ref, val):
        vb = val.astype(bf16)
        for h in range(N_HEADS):
            ref[h] = vb[:, h * HEAD_DIM:(h + 1) * HEAD_DIM]

    c0 = 0
    q_a = _dot(xb, w_ref[:, c0:c0 + W])
    q_a = _rope(_head_rmsnorm(q_a, gqa_ref[...], bd), cos, sin)
    heads_out(qa_ref, q_a * ATTN_SCALE)
    k_a = _dot(xb, w_ref[:, c0 + W:c0 + 2 * W])
    k_a = _rope(_head_rmsnorm(k_a, gka_ref[...], bd), cos, sin)
    akv_ref[:, 0:W] = k_a
    heads_out(ka_ref, k_a)
    v_a = _dot(xb, w_ref[:, c0 + 2 * W:c0 + 3 * W])
    akv_ref[:, W:2 * W] = v_a
    heads_out(va_ref, v_a)
    c0 += 3 * W

    q_i = _rope(_dot(xb, w_ref[:, c0:c0 + W]), cos, sin)
    heads_out(qi_ref, q_i)
    c0 += W

    sm = _dot(xb, w_ref[:, c0:c0 + SMALL_W])
    lane = lax.broadcasted_iota(jnp.int32, sm.shape, 1)
    roped = _rope(sm, cos_s_ref[...], sin_s_ref[...])
    ls = _log_sigmoid(sm + bfr_ref[...])
    aux = jnp.where((lane >= F_OFF) & (lane < F_OFF + N_HEADS), ls, roped)
    aux_ref[...] = aux
    idxk_ref[...] = aux[:, 0:D_IDX]
    ki_ref[...] = aux[:, 0:D_IDX].astype(bf16)
    logf_ref[...] = aux[:, F_OFF:F_OFF + N_HEADS]
    c0 += SMALL_W

    q_b = _head_rmsnorm(_dot(xb, w_ref[:, c0:c0 + W]), gqb_ref[...], bd)
    heads_out(qb_ref, q_b * ATTN_SCALE)
    k_b = _head_rmsnorm(_dot(xb, w_ref[:, c0 + W:c0 + 2 * W]), gkb_ref[...], bd)
    bkv_ref[:, 0:W] = k_b
    heads_out(kb_ref, k_b)
    v_b = _dot(xb, w_ref[:, c0 + 2 * W:c0 + 3 * W])
    bkv_ref[:, W:2 * W] = v_b
    heads_out(vb_ref, v_b)
    c0 += 3 * W

    D = ga_ref.shape[1]
    ga_ref[...] = _dot(xb, w_ref[:, c0:c0 + D])
    gb_ref[...] = _dot(xb, w_ref[:, c0 + D:c0 + 2 * D])

    logft_ref[...] = _log_sigmoid(_nt_dot(wft_ref[...], xb) + bfc_ref[...])


def _const_spec(shape):
    nd = len(shape)
    return pl.BlockSpec(shape, lambda *_: (0,) * nd, pipeline_mode=pl.Buffered(1))


def _proj(x, pos_tables, wts, n_pos_tiles):
    N, D = x.shape
    W = N_HEADS * HEAD_DIM
    tm = ROW_TILE
    cos, sin, cos_s, sin_s = pos_tables
    row = lambda w: pl.BlockSpec((tm, w), lambda i: (i, 0))
    tab = lambda w: pl.BlockSpec((tm, w), lambda i: (i % n_pos_tiles, 0))
    hm = pl.BlockSpec((N_HEADS, tm, HEAD_DIM), lambda i: (0, i, 0))
    consts = [wts["n1"], wts["w_all"], wts["wft"], wts["bf_col"], wts["bf_row"], wts["gqa"], wts["gka"],
              wts["gqb"], wts["gkb"]]
    in_specs = ([row(D)] + [_const_spec(c.shape) for c in consts]
                + [tab(W), tab(W), tab(SMALL_W), tab(SMALL_W), _const_spec(wts["bd"].shape)])
    sds = jax.ShapeDtypeStruct
    out_shape = [sds((N, 2 * W), f32), sds((N, D_IDX), f32), sds((N, 2 * W), f32), sds((N, N_HEADS), f32)]
    out_specs = [row(2 * W), row(D_IDX), row(2 * W), row(N_HEADS)]
    out_shape += [sds((N_HEADS, N, HEAD_DIM), bf16)] * 7
    out_specs += [hm] * 7
    out_shape += [sds((N, SMALL_W), f32), sds((N, D_IDX), bf16), sds((N_HEADS, N), f32), sds((N, D), f32),
                  sds((N, D), f32)]
    out_specs += [row(SMALL_W), row(D_IDX), pl.BlockSpec((N_HEADS, tm), lambda i: (0, i)), row(D), row(D)]
    names = ["akv", "idxk", "bkv", "logf", "qa", "ka", "va", "qi", "qb", "kb", "vb", "aux", "ki", "logft", "ga", "gb"]
    outs = pl.pallas_call(
        _proj_kernel,
        grid=(N // tm,),
        in_specs=in_specs,
        out_specs=out_specs,
        out_shape=out_shape,
        compiler_params=pltpu.CompilerParams(dimension_semantics=("arbitrary",), vmem_limit_bytes=VMEM_LIMIT),
        name="proj",
    )(x, *consts, cos, sin, cos_s, sin_s, wts["bd"])
    return dict(zip(names, outs))


def _float_key(score):
    bits = lax.bitcast_convert_type(score + 0.0, jnp.int32)
    return jnp.where(bits < 0, bits ^ jnp.int32(0x7FFFFFFF), bits)


def _count(mask):
    return jnp.sum(jnp.where(mask, 1.0, 0.0), axis=1, keepdims=True)


def _topk_mask(key_ref, k, idx_bits):
    R, L = key_ref.shape
    kf = jnp.float32(k)

    zero = jnp.zeros((R, 1), jnp.int32)
    t0 = jnp.where(_count(key_ref[...] >= zero) >= kf, zero, jnp.full((R, 1), INT_MIN, jnp.int32))

    def vbody(i, t):
        cand = t | jnp.left_shift(jnp.int32(1), 30 - i)
        return jnp.where(_count(key_ref[...] >= cand) >= kf, cand, t)

    thr = lax.fori_loop(0, 31, vbody, t0)

    key = key_ref[...]
    need = kf - _count(key > thr)
    eq = key == thr
    lane = lax.broadcasted_iota(jnp.int32, (R, L), 1)

    def ibody(i, c):
        lo, hi = c
        mid = (lo + hi) >> 1
        ok = _count((key_ref[...] == thr) & (lane <= mid)) >= need
        return jnp.where(ok, lo, mid), jnp.where(ok, mid, hi)

    lo0 = jnp.full((R, 1), -1, jnp.int32)
    hi0 = jnp.full((R, 1), L - 1, jnp.int32)
    _, cut = lax.fori_loop(0, idx_bits, ibody, (lo0, hi0))
    return (key > thr) | (eq & (lane <= cut))


def _softmax_pv(s, v):
    m = jnp.max(s, axis=1, keepdims=True)
    p = jnp.exp(s - m)
    l = jnp.sum(p, axis=1, keepdims=True)
    return _dot(p.astype(bf16), v) / l


def _mixer_a_prompt_kernel(topk, idx_bits, qi_ref, aux_ref, ki_ref, qa_ref, ka_ref, va_ref, o_ref, key_ref, bias_ref):
    i = pl.program_id(1)
    tq = qi_ref.shape[1]
    S = ki_ref.shape[0]
    ki = ki_ref[...]
    aux = aux_ref[...]
    score = jnp.zeros((tq, S), f32)
    for h in range(N_HEADS):
        dots = _nt_dot(qi_ref[h], ki)
        w_h = aux[:, WI_OFF + h:WI_OFF + h + 1] * IDX_SCALE
        score = score + jnp.maximum(dots, 0.0) * w_h
    qpos = i * tq + lax.broadcasted_iota(jnp.int32, (tq, S), 0)
    kpos = lax.broadcasted_iota(jnp.int32, (tq, S), 1)
    causal = kpos <= qpos
    key_ref[...] = _float_key(jnp.where(causal, score, -jnp.inf))
    sel = _topk_mask(key_ref, topk, idx_bits)
    bias_ref[...] = jnp.where(sel & causal, 0.0, -jnp.inf)
    outs = []
    for h in range(N_HEADS):
        s = _nt_dot(qa_ref[h], ka_ref[h]) + bias_ref[...]
        outs.append(_softmax_pv(s, va_ref[h]))
    o_ref[...] = jnp.concatenate(outs, axis=1).astype(bf16)


def _mixer_a_prompt(p, B, S):
    N = B * S
    W = N_HEADS * HEAD_DIM
    tq = Q_BLOCK
    nq = S // tq
    topk = min(TOPK_MAX, S // 4)
    idx_bits = int(np.ceil(np.log2(S)))
    qblk = pl.BlockSpec((N_HEADS, tq, HEAD_DIM), lambda b, i: (0, b * nq + i, 0))
    seq = pl.BlockSpec((N_HEADS, S, HEAD_DIM), lambda b, i: (0, b, 0))
    return pl.pallas_call(
        functools.partial(_mixer_a_prompt_kernel, topk, idx_bits),
        grid=(B, nq),
        in_specs=[qblk, pl.BlockSpec((tq, SMALL_W), lambda b, i: (b * nq + i, 0)),
                  pl.BlockSpec((S, D_IDX), lambda b, i: (b, 0)), qblk, seq, seq],
        out_specs=pl.BlockSpec((tq, W), lambda b, i: (b * nq + i, 0)),
        out_shape=jax.ShapeDtypeStruct((N, W), bf16),
        scratch_shapes=[pltpu.VMEM((tq, S), jnp.int32), pltpu.VMEM((tq, S), f32)],
        compiler_params=pltpu.CompilerParams(dimension_semantics=("arbitrary", "arbitrary"),
                                             vmem_limit_bytes=VMEM_LIMIT),
        name="mixer_a_prompt",
    )(p["qi"], p["aux"], p["ki"], p["qa"], p["ka"], p["va"])


def _suffix_bias(logf_row, carry, strict_upper):
    a1, a2, a3 = _split3(logf_row)
    inner = _dot(a1, strict_upper) + _dot(a2, strict_upper) + _dot(a3, strict_upper)
    return inner + carry, carry + jnp.sum(logf_row, axis=1, keepdims=True)


def _strict_lower_ones(n):
    r = lax.broadcasted_iota(jnp.int32, (n, n), 0)
    c = lax.broadcasted_iota(jnp.int32, (n, n), 1)
    return jnp.where(r > c, 1.0, 0.0).astype(bf16)


def _mixer_b_prompt_kernel(logft_ref, qb_ref, kb_ref, vb_ref, o_ref, g_ref):
    i = pl.program_id(1)
    tq = qb_ref.shape[1]
    S = kb_ref.shape[1]

    @pl.when(i == 0)
    def _():
        u = _strict_lower_ones(LANES)
        carry = jnp.zeros((N_HEADS, 1), f32)
        for c in reversed(range(S // LANES)):
            g, carry = _suffix_bias(logft_ref[:, c * LANES:(c + 1) * LANES], carry, u)
            g_ref[:, c * LANES:(c + 1) * LANES] = g

    qpos = i * tq + lax.broadcasted_iota(jnp.int32, (tq, S), 0)
    kpos = lax.broadcasted_iota(jnp.int32, (tq, S), 1)
    causal = kpos <= qpos
    outs = []
    for h in range(N_HEADS):
        s = _nt_dot(qb_ref[h], kb_ref[h]) + g_ref[h:h + 1, :]
        s = jnp.where(causal, s, -jnp.inf)
        outs.append(_softmax_pv(s, vb_ref[h]))
    o_ref[...] = jnp.concatenate(outs, axis=1).astype(bf16)


def _mixer_b_prompt(p, B, S):
    N = B * S
    W = N_HEADS * HEAD_DIM
    tq = Q_BLOCK
    nq = S // tq
    qblk = pl.BlockSpec((N_HEADS, tq, HEAD_DIM), lambda b, i: (0, b * nq + i, 0))
    seq = pl.BlockSpec((N_HEADS, S, HEAD_DIM), lambda b, i: (0, b, 0))
    return pl.pallas_call(
        _mixer_b_prompt_kernel,
        grid=(B, nq),
        in_specs=[pl.BlockSpec((N_HEADS, S), lambda b, i: (0, b)), qblk, seq, seq],
        out_specs=pl.BlockSpec((tq, W), lambda b, i: (b * nq + i, 0)),
        out_shape=jax.ShapeDtypeStruct((N, W), bf16),
        scratch_shapes=[pltpu.VMEM((N_HEADS, S), f32)],
        compiler_params=pltpu.CompilerParams(dimension_semantics=("arbitrary", "arbitrary"),
                                             vmem_limit_bytes=VMEM_LIMIT),
        name="mixer_b_prompt",
    )(p["logft"], p["qb"], p["kb"], p["vb"])


def _merge_ffn_kernel(n_ff_chunks, x_ref, oa_ref, ob_ref, ga_ref, gb_ref, wua_ref, wub_ref, wo_ref, n2_ref,
                      wg_ref, wu_ref, wout_ref, y_ref):
    u_a = _dot(oa_ref[...], wua_ref[...])
    u_b = _dot(ob_ref[...], wub_ref[...])
    m = _sigmoid(ga_ref[...]) * u_a + _sigmoid(gb_ref[...]) * u_b
    x1 = x_ref[...] + _dot(m.astype(bf16), wo_ref[...])
    hn = _rmsnorm_rows(x1, n2_ref[...]).astype(bf16)
    d_ff = wg_ref.shape[1]
    cw = d_ff // n_ff_chunks
    acc = x1
    for c in range(n_ff_chunks):
        gate = _dot(hn, wg_ref[:, c * cw:(c + 1) * cw])
        up = _dot(hn, wu_ref[:, c * cw:(c + 1) * cw])
        act = gate * _sigmoid(gate) * up
        acc = acc + _dot(act.astype(bf16), wout_ref[c * cw:(c + 1) * cw, :])
    y_ref[...] = acc


def _merge_ffn(x, o_a, o_b, g_a, g_b, wts):
    N, D = x.shape
    W = N_HEADS * HEAD_DIM
    tm = ROW_TILE
    row = lambda w: pl.BlockSpec((tm, w), lambda i: (i, 0))
    consts = [wts["w_up_a"], wts["w_up_b"], wts["w_o"], wts["n2"], wts["w_gate"], wts["w_upf"], wts["w_out"]]
    d_ff = wts["w_gate"].shape[1]
    n_chunks = 2 if d_ff % (2 * LANES) == 0 else 1
    return pl.pallas_call(
        functools.partial(_merge_ffn_kernel, n_chunks),
        grid=(N // tm,),
        in_specs=[row(D), row(W), row(W), row(D), row(D)] + [_const_spec(c.shape) for c in consts],
        out_specs=row(D),
        out_shape=jax.ShapeDtypeStruct((N, D), f32),
        compiler_params=pltpu.CompilerParams(dimension_semantics=("arbitrary",), vmem_limit_bytes=VMEM_LIMIT),
        name="merge_ffn",
    )(x, o_a, o_b, g_a, g_b, *consts)


PAGES_PER_STEP = 8


def _page_specs(block, n_pages, n_steps, reverse):
    G = PAGES_PER_STEP
    specs = []
    for g in range(G):
        if reverse:
            fn = lambda b, j, pt, g=g: (pt[b * n_pages + (n_steps - 1 - j) * G + g], 0, 0)
        else:
            fn = lambda b, j, pt, g=g: (pt[b * n_pages + j * G + g], 0, 0)
        specs.append(pl.BlockSpec(block, fn))
    return specs


def _sample_index_kernel(topk, idx_bits, n_steps, pt_ref, qi_ref, w_ref, kin_ref, *rest):
    G = PAGES_PER_STEP
    pages = rest[:G]
    bias_ref, score_ref, key_ref = rest[G:]
    j = pl.program_id(1)
    T = score_ref.shape[0]
    qi = qi_ref[0]
    w = w_ref[0] * IDX_SCALE

    def page_score(ki):
        dots = _nt_dot(qi, ki.astype(bf16))
        contrib = jnp.maximum(dots, 0.0) * w
        return jnp.sum(contrib.reshape(N_HEADS, T, PAGE_SIZE), axis=0)

    for g in range(G):
        off = pl.multiple_of((j * G + g) * PAGE_SIZE, PAGE_SIZE)
        score_ref[:, pl.ds(off, PAGE_SIZE)] = page_score(pages[g][0])

    @pl.when(j == n_steps - 1)
    def _():
        past = n_steps * G * PAGE_SIZE
        L = score_ref.shape[1]
        sc = page_score(kin_ref[0])
        t = lax.broadcasted_iota(jnp.int32, (T, PAGE_SIZE), 0)
        jn = lax.broadcasted_iota(jnp.int32, (T, PAGE_SIZE), 1)
        score_ref[:, past:past + PAGE_SIZE] = jnp.where(jn <= t, sc, -jnp.inf)
        key_ref[...] = _float_key(score_ref[...])
        sel = _topk_mask(key_ref, topk, idx_bits)
        kpos = lax.broadcasted_iota(jnp.int32, (T, L), 1)
        qpos = past + lax.broadcasted_iota(jnp.int32, (T, L), 0)
        bias_ref[0] = jnp.where(sel & (kpos <= qpos), 0.0, NEG)


def _sample_index(page_table, qi_rows, w_col, ki_new, pool_ki, T):
    Bd, n_pages = page_table.shape
    G = PAGES_PER_STEP
    n_steps = n_pages // G
    past = n_pages * PAGE_SIZE
    L = past + PAGE_SIZE
    topk = min(TOPK_MAX, (past + T) // 4)
    idx_bits = int(np.ceil(np.log2(L)))
    R = N_HEADS * T
    grid_spec = pltpu.PrefetchScalarGridSpec(
        num_scalar_prefetch=1,
        grid=(Bd, n_steps),
        in_specs=[pl.BlockSpec((1, R, D_IDX), lambda b, j, pt: (b, 0, 0)),
                  pl.BlockSpec((1, R, 1), lambda b, j, pt: (b, 0, 0)),
                  pl.BlockSpec((1, PAGE_SIZE, D_IDX), lambda b, j, pt: (b, 0, 0))]
        + _page_specs((1, PAGE_SIZE, D_IDX), n_pages, n_steps, False),
        out_specs=pl.BlockSpec((1, T, L), lambda b, j, pt: (b, 0, 0)),
        scratch_shapes=[pltpu.VMEM((T, L), f32), pltpu.VMEM((T, L), jnp.int32)],
    )
    return pl.pallas_call(
        functools.partial(_sample_index_kernel, topk, idx_bits, n_steps),
        grid_spec=grid_spec,
        out_shape=jax.ShapeDtypeStruct((Bd, T, L), f32),
        compiler_params=pltpu.CompilerParams(dimension_semantics=("arbitrary", "arbitrary"),
                                             vmem_limit_bytes=VMEM_LIMIT),
        name="sample_index",
    )(page_table.reshape(-1), qi_rows, w_col, ki_new, *([pool_ki] * G))


def _online_block(s, v, m_ref, l_ref, acc_ref):
    m_old = m_ref[...]
    m_new = jnp.maximum(m_old, jnp.max(s, axis=1, keepdims=True))
    alpha = jnp.exp(m_old - m_new)
    p = jnp.exp(s - m_new)
    l_ref[...] = alpha * l_ref[...] + jnp.sum(p, axis=1, keepdims=True)
    acc_ref[...] = alpha * acc_ref[...] + _dot(p.astype(bf16), v)
    m_ref[...] = m_new


def _finish_heads(o_ref, l_ref, acc_ref, T):
    R, W = acc_ref.shape
    rh = lax.broadcasted_iota(jnp.int32, (R, W), 0) // T
    ch = lax.broadcasted_iota(jnp.int32, (R, W), 1) // HEAD_DIM
    o = jnp.where(rh == ch, acc_ref[...] / l_ref[...], 0.0)
    o_ref[0] = jnp.sum(o.reshape(N_HEADS, T, W), axis=0).astype(bf16)


def _init_online(m_ref, l_ref, acc_ref):
    m_ref[...] = jnp.full(m_ref.shape, NEG, f32)
    l_ref[...] = jnp.zeros(l_ref.shape, f32)
    acc_ref[...] = jnp.zeros(acc_ref.shape, f32)


def _sample_a_kernel(n_steps, pt_ref, q_ref, bias_ref, kvn_ref, *rest):
    G = PAGES_PER_STEP
    pages = rest[:G]
    o_ref, m_ref, l_ref, acc_ref = rest[G:]
    j = pl.program_id(1)
    T = bias_ref.shape[1]
    W = N_HEADS * HEAD_DIM
    q = q_ref[0]

    @pl.when(j == 0)
    def _():
        _init_online(m_ref, l_ref, acc_ref)

    def block(kv, bias):
        kvb = kv.astype(bf16)
        s = _nt_dot(q, kvb[:, 0:W]) + jnp.tile(bias, (N_HEADS, 1))
        _online_block(s, kvb[:, W:2 * W], m_ref, l_ref, acc_ref)

    for g in range(G):
        off = pl.multiple_of((j * G + g) * PAGE_SIZE, PAGE_SIZE)
        block(pages[g][0], bias_ref[0, :, pl.ds(off, PAGE_SIZE)])

    @pl.when(j == n_steps - 1)
    def _():
        past = n_steps * G * PAGE_SIZE
        block(kvn_ref[0], bias_ref[0, :, past:past + PAGE_SIZE])
        _finish_heads(o_ref, l_ref, acc_ref, T)


def _sample_a(page_table, q_bd, bias, kv_new, pool_kv, T):
    Bd, n_pages = page_table.shape
    G = PAGES_PER_STEP
    n_steps = n_pages // G
    W = N_HEADS * HEAD_DIM
    R = N_HEADS * T
    L = bias.shape[2]
    grid_spec = pltpu.PrefetchScalarGridSpec(
        num_scalar_prefetch=1,
        grid=(Bd, n_steps),
        in_specs=[pl.BlockSpec((1, R, W), lambda b, j, pt: (b, 0, 0)),
                  pl.BlockSpec((1, T, L), lambda b, j, pt: (b, 0, 0)),
                  pl.BlockSpec((1, PAGE_SIZE, 2 * W), lambda b, j, pt: (b, 0, 0))]
        + _page_specs((1, PAGE_SIZE, 2 * W), n_pages, n_steps, False),
        out_specs=pl.BlockSpec((1, T, W), lambda b, j, pt: (b, 0, 0)),
        scratch_shapes=[pltpu.VMEM((R, 1), f32), pltpu.VMEM((R, 1), f32), pltpu.VMEM((R, W), f32)],
    )
    return pl.pallas_call(
        functools.partial(_sample_a_kernel, n_steps),
        grid_spec=grid_spec,
        out_shape=jax.ShapeDtypeStruct((Bd, T, W), bf16),
        compiler_params=pltpu.CompilerParams(dimension_semantics=("arbitrary", "arbitrary"),
                                             vmem_limit_bytes=VMEM_LIMIT),
        name="sample_a",
    )(page_table.reshape(-1), q_bd, bias, kv_new, *([pool_kv] * G))


def _sample_b_kernel(n_steps, pt_ref, q_ref, kvn_ref, lfn_ref, *rest):
    G = PAGES_PER_STEP
    kv_pages = rest[:G]
    lf_pages = rest[G:2 * G]
    o_ref, m_ref, l_ref, acc_ref, carry_ref, pad_ref = rest[2 * G:]
    j = pl.program_id(1)
    R = q_ref.shape[1]
    T = R // N_HEADS
    W = N_HEADS * HEAD_DIM
    q = q_ref[0]
    u = _strict_lower_ones(PAGE_SIZE)

    def block(kv, logf, mask):
        pad_ref[:, 0:N_HEADS] = logf
        logf_row = pad_ref[...].T[0:N_HEADS, :]
        g, carry = _suffix_bias(logf_row, carry_ref[...], u)
        carry_ref[...] = carry
        bias = jnp.concatenate([jnp.broadcast_to(g[h:h + 1, :], (T, PAGE_SIZE)) for h in range(N_HEADS)], axis=0)
        kvb = kv.astype(bf16)
        s = _nt_dot(q, kvb[:, 0:W]) + bias
        if mask is not None:
            s = jnp.where(mask, s, NEG)
        _online_block(s, kvb[:, W:2 * W], m_ref, l_ref, acc_ref)

    @pl.when(j == 0)
    def _():
        _init_online(m_ref, l_ref, acc_ref)
        carry_ref[...] = jnp.zeros(carry_ref.shape, f32)
        pad_ref[...] = jnp.zeros(pad_ref.shape, f32)
        t = lax.broadcasted_iota(jnp.int32, (R, PAGE_SIZE), 0) % T
        jn = lax.broadcasted_iota(jnp.int32, (R, PAGE_SIZE), 1)
        block(kvn_ref[0], lfn_ref[0], jn <= t)

    for g in reversed(range(G)):
        block(kv_pages[g][0], lf_pages[g][0], None)

    @pl.when(j == n_steps - 1)
    def _():
        _finish_heads(o_ref, l_ref, acc_ref, T)


def _sample_b(page_table, q_bd, kv_new, logf_new, pool_kv, pool_logf, T):
    Bd, n_pages = page_table.shape
    G = PAGES_PER_STEP
    n_steps = n_pages // G
    W = N_HEADS * HEAD_DIM
    R = N_HEADS * T
    grid_spec = pltpu.PrefetchScalarGridSpec(
        num_scalar_prefetch=1,
        grid=(Bd, n_steps),
        in_specs=[pl.BlockSpec((1, R, W), lambda b, j, pt: (b, 0, 0)),
                  pl.BlockSpec((1, PAGE_SIZE, 2 * W), lambda b, j, pt: (b, 0, 0)),
                  pl.BlockSpec((1, PAGE_SIZE, N_HEADS), lambda b, j, pt: (b, 0, 0))]
        + _page_specs((1, PAGE_SIZE, 2 * W), n_pages, n_steps, True)
        + _page_specs((1, PAGE_SIZE, N_HEADS), n_pages, n_steps, True),
        out_specs=pl.BlockSpec((1, T, W), lambda b, j, pt: (b, 0, 0)),
        scratch_shapes=[pltpu.VMEM((R, 1), f32), pltpu.VMEM((R, 1), f32), pltpu.VMEM((R, W), f32),
                        pltpu.VMEM((N_HEADS, 1), f32), pltpu.VMEM((PAGE_SIZE, PAGE_SIZE), f32)],
    )
    return pl.pallas_call(
        functools.partial(_sample_b_kernel, n_steps),
        grid_spec=grid_spec,
        out_shape=jax.ShapeDtypeStruct((Bd, T, W), bf16),
        compiler_params=pltpu.CompilerParams(dimension_semantics=("arbitrary", "arbitrary"),
                                             vmem_limit_bytes=VMEM_LIMIT),
        name="sample_b",
    )(page_table.reshape(-1), q_bd, kv_new, logf_new, *([pool_kv] * G), *([pool_logf] * G))


def _rope_tables(pos):
    half = HEAD_DIM // 2
    inv = jnp.power(jnp.float32(ROPE_THETA), -jnp.arange(half, dtype=f32) / half)
    ang = pos.astype(f32)[:, None] * inv[None, :]
    cos = jnp.cos(ang)
    sin = jnp.sin(ang)
    cos64 = jnp.concatenate([cos, cos], axis=1)
    sin64 = jnp.concatenate([-sin, sin], axis=1)
    n = pos.shape[0]
    cos_s = jnp.concatenate([cos64, jnp.ones((n, SMALL_W - HEAD_DIM), f32)], axis=1)
    sin_s = jnp.concatenate([sin64, jnp.zeros((n, SMALL_W - HEAD_DIM), f32)], axis=1)
    return jnp.tile(cos64, (1, N_HEADS)), jnp.tile(sin64, (1, N_HEADS)), cos_s, sin_s


def _layer_weights(l, norm1, w_in, b_f, qn_a, kn_a, qn_b, kn_b, w_up_a, w_up_b, w_o, norm2, w_ffn_in, w_ffn_out):
    D = w_in.shape[1]
    W = N_HEADS * HEAD_DIM
    w = w_in[l]
    splits = (3 * W, W, D_IDX, N_HEADS, 3 * W, N_HEADS, 2 * D)
    cuts = np.cumsum(splits)[:-1]
    w_a, w_qi, w_ki, w_wi, w_b, w_f, w_g = jnp.split(w, [int(c) for c in cuts], axis=1)
    pad = jnp.zeros((D, SMALL_W - D_IDX - 2 * N_HEADS), w.dtype)
    w_all = jnp.concatenate([w_a, w_qi, w_ki, w_wi, w_f, pad, w_b, w_g], axis=1).astype(bf16)
    bf_row = jnp.zeros((1, SMALL_W), f32).at[0, F_OFF:F_OFF + N_HEADS].set(b_f[l])
    d_ff = w_ffn_out.shape[1]
    tile_g = lambda g: jnp.tile(g[l].astype(f32), N_HEADS)[None, :]
    r = np.arange(W) // HEAD_DIM
    bd = jnp.asarray((r[:, None] == r[None, :]).astype(np.float32)).astype(bf16)
    return {
        "n1": norm1[l][None, :], "w_all": w_all, "wft": w_f.T.astype(bf16), "bf_col": b_f[l][:, None],
        "bf_row": bf_row, "gqa": tile_g(qn_a), "gka": tile_g(kn_a), "gqb": tile_g(qn_b), "gkb": tile_g(kn_b),
        "bd": bd,
        "w_up_a": w_up_a[l].astype(bf16), "w_up_b": w_up_b[l].astype(bf16), "w_o": w_o[l].astype(bf16),
        "n2": norm2[l][None, :], "w_gate": w_ffn_in[l][:, :d_ff].astype(bf16),
        "w_upf": w_ffn_in[l][:, d_ff:].astype(bf16), "w_out": w_ffn_out[l].astype(bf16),
    }


def _rows_by_head(x_hm, Bd, T):
    return jnp.transpose(x_hm.reshape(N_HEADS, Bd, T, HEAD_DIM), (1, 0, 2, 3))


def _block_diag_q(x_hm, Bd, T):
    q = _rows_by_head(x_hm, Bd, T)
    eye = jnp.eye(N_HEADS, dtype=q.dtype)
    return jnp.einsum("bhtd,hg->bhtgd", q, eye).reshape(Bd, N_HEADS * T, N_HEADS * HEAD_DIM)


def _pad_rows(x, Bd, T):
    x = x.reshape(Bd, T, x.shape[-1])
    return jnp.pad(x, ((0, 0), (0, PAGE_SIZE - T), (0, 0)))


def kernel(x_prompt, x_sample, cache_a_kv, cache_idx_k, cache_b_kv, cache_b_logf, page_table, norm1, w_in, b_f,
           qn_a, kn_a, qn_b, kn_b, w_up_a, w_up_b, w_o, norm2, w_ffn_in, w_ffn_out):
    B, S, D = x_prompt.shape
    Bd, T, _ = x_sample.shape
    depth = w_in.shape[0]
    n_pool = cache_a_kv.shape[1]
    n_pages = page_table.shape[1]
    past = n_pages * PAGE_SIZE
    W = N_HEADS * HEAD_DIM
    assert S % ROW_TILE == 0 and (Bd * T) % ROW_TILE == 0 and ROW_TILE % T == 0
    assert n_pages % PAGES_PER_STEP == 0 and T <= 8

    tab_p = _rope_tables(jnp.arange(S))
    tab_s = _rope_tables(jnp.tile(past + jnp.arange(T), ROW_TILE // T))

    h_p = x_prompt.reshape(B * S, D)
    h_s = x_sample.reshape(Bd * T, D)
    outs = [[] for _ in range(8)]
    for l in range(depth):
        wts = _layer_weights(l, norm1, w_in, b_f, qn_a, kn_a, qn_b, kn_b, w_up_a, w_up_b, w_o, norm2, w_ffn_in,
                             w_ffn_out)
        p = _proj(h_p, tab_p, wts, S // ROW_TILE)
        o_a = _mixer_a_prompt(p, B, S)
        o_b = _mixer_b_prompt(p, B, S)
        h_p = _merge_ffn(h_p, o_a, o_b, p["ga"], p["gb"], wts)
        outs[0].append(p["akv"].reshape(B, S, 2, N_HEADS, HEAD_DIM))
        outs[1].append(p["idxk"].reshape(B, S, D_IDX))
        outs[2].append(p["bkv"].reshape(B, S, 2, N_HEADS, HEAD_DIM))
        outs[3].append(p["logf"].reshape(B, S, N_HEADS))

        s = _proj(h_s, tab_s, wts, 1)
        qi_rows = _rows_by_head(s["qi"], Bd, T).reshape(Bd, N_HEADS * T, D_IDX)
        w_col = jnp.transpose(s["aux"][:, WI_OFF:WI_OFF + N_HEADS].reshape(Bd, T, N_HEADS), (0, 2, 1))
        w_col = w_col.reshape(Bd, N_HEADS * T, 1)
        bias = _sample_index(page_table, qi_rows, w_col, _pad_rows(s["idxk"], Bd, T),
                             cache_idx_k[l].reshape(n_pool, PAGE_SIZE, D_IDX), T)
        o_a = _sample_a(page_table, _block_diag_q(s["qa"], Bd, T), bias, _pad_rows(s["akv"], Bd, T),
                        cache_a_kv[l].reshape(n_pool, PAGE_SIZE, 2 * W), T)
        o_b = _sample_b(page_table, _block_diag_q(s["qb"], Bd, T), _pad_rows(s["bkv"], Bd, T),
                        _pad_rows(s["logf"], Bd, T), cache_b_kv[l].reshape(n_pool, PAGE_SIZE, 2 * W),
                        cache_b_logf[l].reshape(n_pool, PAGE_SIZE, N_HEADS), T)
        h_s = _merge_ffn(h_s, o_a.reshape(Bd * T, W), o_b.reshape(Bd * T, W), s["ga"], s["gb"], wts)
        outs[4].append(s["akv"].reshape(Bd, T, 2, N_HEADS, HEAD_DIM))
        outs[5].append(s["idxk"].reshape(Bd, T, D_IDX))
        outs[6].append(s["bkv"].reshape(Bd, T, 2, N_HEADS, HEAD_DIM))
        outs[7].append(s["logf"].reshape(Bd, T, N_HEADS))

    return (h_p.reshape(B, S, D), h_s.reshape(Bd, T, D)) + tuple(jnp.stack(o) for o in outs)
```

```python
import functools

import jax
import jax.numpy as jnp
import numpy as np
from jax import lax
from jax.experimental import pallas as pl
from jax.experimental.pallas import tpu as pltpu

HEAD_DIM = 64
N_HEADS = 8
D_IDX = 64
PAGE_SIZE = 128
TOPK_MAX = 256
ROPE_THETA = 10000.0
EPS = 1e-6
ATTN_SCALE = HEAD_DIM ** -0.5
IDX_SCALE = (N_HEADS * D_IDX) ** -0.5
Q_BLOCK = 128
KEY_CHUNK = 512
ROW_TILE = 256
LANES = 128
SMALL_ROWS = 16
NEG = -1e30
VMEM_LIMIT = 56 * 1024 * 1024
INT_MIN = -2 ** 31
PAGES_PER_STEP = 8

bf16 = jnp.bfloat16
f32 = jnp.float32
W_HEADS = N_HEADS * HEAD_DIM


def _nt_dot(a, b):
    return lax.dot_general(a, b, (((1,), (1,)), ((), ())), preferred_element_type=f32)


def _dot(a, b):
    return jnp.dot(a, b, preferred_element_type=f32)


def _split3(x):
    a1 = x.astype(bf16)
    r1 = x - a1.astype(f32)
    a2 = r1.astype(bf16)
    a3 = (r1 - a2.astype(f32)).astype(bf16)
    return a1, a2, a3


def _log_sigmoid(x):
    return -(jnp.maximum(-x, 0.0) + jnp.log1p(jnp.exp(-jnp.abs(x))))


def _sigmoid(x):
    return 1.0 / (1.0 + jnp.exp(-x))


def _rmsnorm_rows(x, g):
    return x * lax.rsqrt(jnp.mean(x * x, axis=-1, keepdims=True) + EPS) * g


def _head_rmsnorm(x, g, bd):
    x2 = x * x
    hi = x2.astype(bf16)
    lo = (x2 - hi.astype(f32)).astype(bf16)
    ss = _dot(hi, bd) + _dot(lo, bd)
    return x * lax.rsqrt(ss * (1.0 / HEAD_DIM) + EPS) * g


def _rope(x, cos, sin_signed):
    w = x.shape[-1]
    half = HEAD_DIM // 2
    lane = lax.broadcasted_iota(jnp.int32, x.shape, 1)
    first = (lane & half) == 0
    partner = jnp.where(first, pltpu.roll(x, w - half, 1), pltpu.roll(x, half, 1))
    return x * cos + partner * sin_signed


def _head_rmsnorm_t(x, g):
    ms = jnp.mean(x * x, axis=1, keepdims=True)
    return x * lax.rsqrt(ms + EPS) * g[None]


def _rope_t(x, cos, sin):
    half = HEAD_DIM // 2
    x1 = x[:, :half, :]
    x2 = x[:, half:, :]
    return jnp.concatenate([x1 * cos[None] - x2 * sin[None], x2 * cos[None] + x1 * sin[None]], axis=1)


def _proj_kernel(x_ref, n1_ref, wr_ref, wt_ref, bfc_ref, gqa_ref, gka_ref, gqb_ref, gkb_ref,
                 cos_ref, sin_ref, cost_ref, sint_ref, bd_ref,
                 akv_ref, idxk_ref, bkv_ref, logf_ref,
                 qa_ref, qi_ref, qb_ref, aux_ref, ga_ref, gb_ref,
                 kat_ref, vat_ref, kbt_ref, vbt_ref, kit_ref):
    W = W_HEADS
    xn = _rmsnorm_rows(x_ref[...], n1_ref[...])
    xb = xn.astype(bf16)
    bd = bd_ref[...]
    cos = cos_ref[...]
    sin = sin_ref[...]
    cos_t = cost_ref[...]
    sin_t = sint_ref[...]
    n = xb.shape[0]

    def heads_out(ref, val):
        vb = val.astype(bf16)
        for h in range(N_HEADS):
            ref[h] = vb[:, h * HEAD_DIM:(h + 1) * HEAD_DIM]

    q_a = _dot(xb, wr_ref[:, 0:W])
    heads_out(qa_ref, _rope(_head_rmsnorm(q_a, gqa_ref[...], bd), cos, sin) * ATTN_SCALE)
    heads_out(qi_ref, _rope(_dot(xb, wr_ref[:, W:2 * W]), cos, sin))
    q_b = _dot(xb, wr_ref[:, 2 * W:3 * W])
    heads_out(qb_ref, _head_rmsnorm(q_b, gqb_ref[...], bd) * ATTN_SCALE)
    c0 = 3 * W
    aux_ref[...] = _dot(xb, wr_ref[:, c0:c0 + LANES])
    c0 += LANES
    D = ga_ref.shape[1]
    ga_ref[...] = _dot(xb, wr_ref[:, c0:c0 + D])
    gb_ref[...] = _dot(xb, wr_ref[:, c0 + D:c0 + 2 * D])

    def t_proj(r0, rows):
        return _nt_dot(wt_ref[r0:r0 + rows, :], xb)

    k_a = _rope_t(_head_rmsnorm_t(t_proj(0, W).reshape(N_HEADS, HEAD_DIM, n), gka_ref[...]), cos_t, sin_t)
    k_a = k_a.reshape(W, n)
    akv_ref[0, 0:W, :] = k_a
    kat_ref[...] = k_a.astype(bf16)
    v_a = t_proj(W, W)
    akv_ref[0, W:2 * W, :] = v_a
    vat_ref[...] = v_a.astype(bf16)
    k_b = _head_rmsnorm_t(t_proj(2 * W, W).reshape(N_HEADS, HEAD_DIM, n), gkb_ref[...]).reshape(W, n)
    bkv_ref[0, 0:W, :] = k_b
    kbt_ref[...] = k_b.astype(bf16)
    v_b = t_proj(3 * W, W)
    bkv_ref[0, W:2 * W, :] = v_b
    vbt_ref[...] = v_b.astype(bf16)
    k_i = _rope_t(t_proj(4 * W, D_IDX)[None], cos_t, sin_t)[0]
    idxk_ref[0] = k_i
    kit_ref[...] = k_i.astype(bf16)
    f = t_proj(4 * W + D_IDX, SMALL_ROWS)[0:N_HEADS, :]
    logf_ref[0] = _log_sigmoid(f + bfc_ref[...])


def _const_spec(shape):
    nd = len(shape)
    return pl.BlockSpec(shape, lambda *_: (0,) * nd, pipeline_mode=pl.Buffered(1))


def _proj(x, rows_per_batch, tables, wts):
    N, D = x.shape
    W = W_HEADS
    tm = ROW_TILE
    rpb = rows_per_batch
    nb = N // rpb
    nt = rpb // tm
    cos, sin, cos_t, sin_t = tables
    row = lambda w: pl.BlockSpec((tm, w), lambda i: (i, 0))
    tab = lambda w: pl.BlockSpec((tm, w), lambda i: (i % nt, 0))
    tab_t = pl.BlockSpec((HEAD_DIM // 2, tm), lambda i: (0, i % nt))
    hm = pl.BlockSpec((N_HEADS, tm, HEAD_DIM), lambda i: (0, i, 0))
    fm = lambda r: pl.BlockSpec((1, r, tm), lambda i: (i // nt, 0, i % nt))
    fm2 = lambda r: pl.BlockSpec((r, tm), lambda i: (0, i))
    consts = [wts["n1"], wts["w_row"], wts["w_t"], wts["bf_col"], wts["gqa"], wts["gka"], wts["gqb"], wts["gkb"]]
    in_specs = ([row(D)] + [_const_spec(c.shape) for c in consts]
                + [tab(W), tab(W), tab_t, tab_t, _const_spec(wts["bd"].shape)])
    sds = jax.ShapeDtypeStruct
    out_shape = [sds((nb, 2 * W, rpb), f32), sds((nb, D_IDX, rpb), f32), sds((nb, 2 * W, rpb), f32),
                 sds((nb, N_HEADS, rpb), f32)]
    out_specs = [fm(2 * W), fm(D_IDX), fm(2 * W), fm(N_HEADS)]
    out_shape += [sds((N_HEADS, N, HEAD_DIM), bf16)] * 3 + [sds((N, LANES), f32), sds((N, D), f32), sds((N, D), f32)]
    out_specs += [hm] * 3 + [row(LANES), row(D), row(D)]
    out_shape += [sds((W, N), bf16)] * 4 + [sds((D_IDX, N), bf16)]
    out_specs += [fm2(W)] * 4 + [fm2(D_IDX)]
    names = ["akv", "idxk", "bkv", "logf", "qa", "qi", "qb", "aux", "ga", "gb", "kat", "vat", "kbt", "vbt", "kit"]
    outs = pl.pallas_call(
        _proj_kernel,
        grid=(N // tm,),
        in_specs=in_specs,
        out_specs=out_specs,
        out_shape=out_shape,
        compiler_params=pltpu.CompilerParams(dimension_semantics=("arbitrary",), vmem_limit_bytes=VMEM_LIMIT),
        name="proj",
    )(x, *consts, cos, sin, cos_t, sin_t, wts["bd"])
    return dict(zip(names, outs))


def _float_key(score):
    bits = lax.bitcast_convert_type(score + 0.0, jnp.int32)
    return jnp.where(bits < 0, bits ^ jnp.int32(0x7FFFFFFF), bits)


def _count(mask):
    return jnp.sum(jnp.where(mask, 1.0, 0.0), axis=1, keepdims=True)


def _topk_mask(key_ref, k, idx_bits):
    R, L = key_ref.shape
    kf = jnp.float32(k)

    zero = jnp.zeros((R, 1), jnp.int32)
    t0 = jnp.where(_count(key_ref[...] >= zero) >= kf, zero, jnp.full((R, 1), INT_MIN, jnp.int32))

    def vbody(i, t):
        cand = t | jnp.left_shift(jnp.int32(1), 30 - i)
        return jnp.where(_count(key_ref[...] >= cand) >= kf, cand, t)

    thr = lax.fori_loop(0, 31, vbody, t0)

    key = key_ref[...]
    gt = key > thr
    need = kf - _count(gt)
    n_eq = _count(key == thr)
    lane = lax.broadcasted_iota(jnp.int32, (R, L), 1)

    def search():
        def ibody(i, c):
            lo, hi = c
            mid = (lo + hi) >> 1
            ok = _count((key_ref[...] == thr) & (lane <= mid)) >= need
            return jnp.where(ok, lo, mid), jnp.where(ok, mid, hi)

        lo0 = jnp.full((R, 1), -1, jnp.int32)
        hi0 = jnp.full((R, 1), L - 1, jnp.int32)
        return lax.fori_loop(0, idx_bits, ibody, (lo0, hi0))[1]

    cut = lax.cond(jnp.max(n_eq - need) > 0.0, search, lambda: jnp.full((R, 1), L - 1, jnp.int32))
    return gt | ((key_ref[...] == thr) & (lane <= cut))


def _key_classes(S, tq):
    ch = min(KEY_CHUNK, S)
    return list(range(ch, S + 1, ch))


def _for_key_class(i, tq, classes, body):
    need = (i + 1) * tq
    lo = 0
    for n in classes:
        pl.when((need > lo) & (need <= n))(functools.partial(body, n))
        lo = n


def _softmax_pv_t(s, vt):
    m = jnp.max(s, axis=1, keepdims=True)
    p = jnp.exp(s - m)
    l = jnp.sum(p, axis=1, keepdims=True)
    return _nt_dot(p.astype(bf16), vt) / l


def _mixer_a_prompt_kernel(topk, idx_bits, classes, qi_ref, aux_ref, kit_ref, qa_ref, kat_ref, vat_ref, o_ref,
                           key_ref, bias_ref):
    i = pl.program_id(1)
    tq = qi_ref.shape[1]

    def body(n):
        kit = kit_ref[:, 0:n]
        aux = aux_ref[...]
        score = jnp.zeros((tq, n), f32)
        for h in range(N_HEADS):
            dots = _dot(qi_ref[h], kit)
            w_h = aux[:, h:h + 1] * IDX_SCALE
            score = score + jnp.maximum(dots, 0.0) * w_h
        qpos = i * tq + lax.broadcasted_iota(jnp.int32, (tq, n), 0)
        kpos = lax.broadcasted_iota(jnp.int32, (tq, n), 1)
        causal = kpos <= qpos
        keys = key_ref.at[:, 0:n]
        keys[...] = _float_key(jnp.where(causal, score, -jnp.inf))
        sel = _topk_mask(keys, topk, idx_bits)
        bias_ref[:, 0:n] = jnp.where(sel & causal, 0.0, -jnp.inf)
        outs = []
        for h in range(N_HEADS):
            r = slice(h * HEAD_DIM, (h + 1) * HEAD_DIM)
            s = _dot(qa_ref[h], kat_ref[r, 0:n]) + bias_ref[:, 0:n]
            outs.append(_softmax_pv_t(s, vat_ref[r, 0:n]))
        o_ref[...] = jnp.concatenate(outs, axis=1).astype(bf16)

    _for_key_class(i, tq, classes, body)


def _mixer_a_prompt(p, B, S):
    N = B * S
    W = W_HEADS
    tq = Q_BLOCK
    nq = S // tq
    topk = min(TOPK_MAX, S // 4)
    idx_bits = int(np.ceil(np.log2(S)))
    qblk = pl.BlockSpec((N_HEADS, tq, HEAD_DIM), lambda b, i: (0, b * nq + i, 0))
    seq = lambda r: pl.BlockSpec((r, S), lambda b, i: (0, b))
    return pl.pallas_call(
        functools.partial(_mixer_a_prompt_kernel, topk, idx_bits, _key_classes(S, tq)),
        grid=(B, nq),
        in_specs=[qblk, pl.BlockSpec((tq, LANES), lambda b, i: (b * nq + i, 0)), seq(D_IDX), qblk, seq(W), seq(W)],
        out_specs=pl.BlockSpec((tq, W), lambda b, i: (b * nq + i, 0)),
        out_shape=jax.ShapeDtypeStruct((N, W), bf16),
        scratch_shapes=[pltpu.VMEM((tq, S), jnp.int32), pltpu.VMEM((tq, S), f32)],
        compiler_params=pltpu.CompilerParams(dimension_semantics=("arbitrary", "arbitrary"),
                                             vmem_limit_bytes=VMEM_LIMIT),
        name="mixer_a_prompt",
    )(p["qi"], p["aux"], p["kit"], p["qa"], p["kat"], p["vat"])


def _suffix_bias(logf_row, carry, strict_upper):
    a1, a2, a3 = _split3(logf_row)
    inner = _dot(a1, strict_upper) + _dot(a2, strict_upper) + _dot(a3, strict_upper)
    return inner + carry, carry + jnp.sum(logf_row, axis=1, keepdims=True)


def _strict_lower_ones(n):
    r = lax.broadcasted_iota(jnp.int32, (n, n), 0)
    c = lax.broadcasted_iota(jnp.int32, (n, n), 1)
    return jnp.where(r > c, 1.0, 0.0).astype(bf16)


def _mixer_b_prompt_kernel(classes, logf_ref, qb_ref, kbt_ref, vbt_ref, o_ref, g_ref):
    i = pl.program_id(1)
    tq = qb_ref.shape[1]
    S = kbt_ref.shape[1]

    @pl.when(i == 0)
    def _():
        u = _strict_lower_ones(LANES)
        carry = jnp.zeros((N_HEADS, 1), f32)
        for c in reversed(range(S // LANES)):
            g, carry = _suffix_bias(logf_ref[0, :, c * LANES:(c + 1) * LANES], carry, u)
            g_ref[:, c * LANES:(c + 1) * LANES] = g

    def body(n):
        qpos = i * tq + lax.broadcasted_iota(jnp.int32, (tq, n), 0)
        kpos = lax.broadcasted_iota(jnp.int32, (tq, n), 1)
        causal = kpos <= qpos
        outs = []
        for h in range(N_HEADS):
            r = slice(h * HEAD_DIM, (h + 1) * HEAD_DIM)
            s = _dot(qb_ref[h], kbt_ref[r, 0:n]) + g_ref[h:h + 1, 0:n]
            s = jnp.where(causal, s, -jnp.inf)
            outs.append(_softmax_pv_t(s, vbt_ref[r, 0:n]))
        o_ref[...] = jnp.concatenate(outs, axis=1).astype(bf16)

    _for_key_class(i, tq, classes, body)


def _mixer_b_prompt(p, B, S):
    N = B * S
    W = W_HEADS
    tq = Q_BLOCK
    nq = S // tq
    qblk = pl.BlockSpec((N_HEADS, tq, HEAD_DIM), lambda b, i: (0, b * nq + i, 0))
    seq = pl.BlockSpec((W, S), lambda b, i: (0, b))
    return pl.pallas_call(
        functools.partial(_mixer_b_prompt_kernel, _key_classes(S, tq)),
        grid=(B, nq),
        in_specs=[pl.BlockSpec((1, N_HEADS, S), lambda b, i: (b, 0, 0)), qblk, seq, seq],
        out_specs=pl.BlockSpec((tq, W), lambda b, i: (b * nq + i, 0)),
        out_shape=jax.ShapeDtypeStruct((N, W), bf16),
        scratch_shapes=[pltpu.VMEM((N_HEADS, S), f32)],
        compiler_params=pltpu.CompilerParams(dimension_semantics=("arbitrary", "arbitrary"),
                                             vmem_limit_bytes=VMEM_LIMIT),
        name="mixer_b_prompt",
    )(p["logf"], p["qb"], p["kbt"], p["vbt"])


def _merge_ffn_kernel(n_ff_chunks, x_ref, oa_ref, ob_ref, ga_ref, gb_ref, wua_ref, wub_ref, wo_ref, n2_ref,
                      wg_ref, wu_ref, wout_ref, y_ref):
    u_a = _dot(oa_ref[...], wua_ref[...])
    u_b = _dot(ob_ref[...], wub_ref[...])
    m = _sigmoid(ga_ref[...]) * u_a + _sigmoid(gb_ref[...]) * u_b
    x1 = x_ref[...] + _dot(m.astype(bf16), wo_ref[...])
    hn = _rmsnorm_rows(x1, n2_ref[...]).astype(bf16)
    d_ff = wg_ref.shape[1]
    cw = d_ff // n_ff_chunks
    acc = x1
    for c in range(n_ff_chunks):
        gate = _dot(hn, wg_ref[:, c * cw:(c + 1) * cw])
        up = _dot(hn, wu_ref[:, c * cw:(c + 1) * cw])
        act = gate * _sigmoid(gate) * up
        acc = acc + _dot(act.astype(bf16), wout_ref[c * cw:(c + 1) * cw, :])
    y_ref[...] = acc


def _merge_ffn(x, o_a, o_b, g_a, g_b, wts):
    N, D = x.shape
    W = W_HEADS
    tm = ROW_TILE
    row = lambda w: pl.BlockSpec((tm, w), lambda i: (i, 0))
    consts = [wts["w_up_a"], wts["w_up_b"], wts["w_o"], wts["n2"], wts["w_gate"], wts["w_upf"], wts["w_out"]]
    d_ff = wts["w_gate"].shape[1]
    n_chunks = 2 if d_ff % (2 * LANES) == 0 else 1
    return pl.pallas_call(
        functools.partial(_merge_ffn_kernel, n_chunks),
        grid=(N // tm,),
        in_specs=[row(D), row(W), row(W), row(D), row(D)] + [_const_spec(c.shape) for c in consts],
        out_specs=row(D),
        out_shape=jax.ShapeDtypeStruct((N, D), f32),
        compiler_params=pltpu.CompilerParams(dimension_semantics=("arbitrary",), vmem_limit_bytes=VMEM_LIMIT),
        name="merge_ffn",
    )(x, o_a, o_b, g_a, g_b, *consts)


def _page_specs(block, n_pages, n_steps, reverse):
    G = PAGES_PER_STEP
    specs = []
    for g in range(G):
        if reverse:
            fn = lambda b, j, pt, g=g: (pt[b * n_pages + (n_steps - 1 - j) * G + g], 0, 0)
        else:
            fn = lambda b, j, pt, g=g: (pt[b * n_pages + j * G + g], 0, 0)
        specs.append(pl.BlockSpec(block, fn))
    return specs


def _sample_index_kernel(topk, idx_bits, n_steps, pt_ref, qi_ref, w_ref, kin_ref, *rest):
    G = PAGES_PER_STEP
    pages = rest[:G]
    bias_ref, score_ref, key_ref = rest[G:]
    j = pl.program_id(1)
    T = score_ref.shape[0]
    qi = qi_ref[0]
    w = w_ref[0] * IDX_SCALE

    def page_score(kit):
        dots = _dot(qi, kit.astype(bf16))
        contrib = jnp.maximum(dots, 0.0) * w
        return jnp.sum(contrib.reshape(N_HEADS, T, PAGE_SIZE), axis=0)

    for g in range(G):
        off = pl.multiple_of((j * G + g) * PAGE_SIZE, PAGE_SIZE)
        score_ref[:, pl.ds(off, PAGE_SIZE)] = page_score(pages[g][0])

    @pl.when(j == n_steps - 1)
    def _():
        past = n_steps * G * PAGE_SIZE
        L = score_ref.shape[1]
        sc = page_score(kin_ref[0])
        t = lax.broadcasted_iota(jnp.int32, (T, PAGE_SIZE), 0)
        jn = lax.broadcasted_iota(jnp.int32, (T, PAGE_SIZE), 1)
        score_ref[:, past:past + PAGE_SIZE] = jnp.where(jn <= t, sc, -jnp.inf)
        key_ref[...] = _float_key(score_ref[...])
        sel = _topk_mask(key_ref, topk, idx_bits)
        kpos = lax.broadcasted_iota(jnp.int32, (T, L), 1)
        qpos = past + lax.broadcasted_iota(jnp.int32, (T, L), 0)
        bias_ref[0] = jnp.where(sel & (kpos <= qpos), 0.0, NEG)


def _sample_index(page_table, qi_rows, w_col, ki_new, pool_kit, T):
    Bd, n_pages = page_table.shape
    G = PAGES_PER_STEP
    n_steps = n_pages // G
    past = n_pages * PAGE_SIZE
    L = past + PAGE_SIZE
    topk = min(TOPK_MAX, (past + T) // 4)
    idx_bits = int(np.ceil(np.log2(L)))
    R = N_HEADS * T
    grid_spec = pltpu.PrefetchScalarGridSpec(
        num_scalar_prefetch=1,
        grid=(Bd, n_steps),
        in_specs=[pl.BlockSpec((1, R, D_IDX), lambda b, j, pt: (b, 0, 0)),
                  pl.BlockSpec((1, R, 1), lambda b, j, pt: (b, 0, 0)),
                  pl.BlockSpec((1, D_IDX, PAGE_SIZE), lambda b, j, pt: (b, 0, 0))]
        + _page_specs((1, D_IDX, PAGE_SIZE), n_pages, n_steps, False),
        out_specs=pl.BlockSpec((1, T, L), lambda b, j, pt: (b, 0, 0)),
        scratch_shapes=[pltpu.VMEM((T, L), f32), pltpu.VMEM((T, L), jnp.int32)],
    )
    return pl.pallas_call(
        functools.partial(_sample_index_kernel, topk, idx_bits, n_steps),
        grid_spec=grid_spec,
        out_shape=jax.ShapeDtypeStruct((Bd, T, L), f32),
        compiler_params=pltpu.CompilerParams(dimension_semantics=("arbitrary", "arbitrary"),
                                             vmem_limit_bytes=VMEM_LIMIT),
        name="sample_index",
    )(page_table.reshape(-1), qi_rows, w_col, ki_new, *([pool_kit] * G))


def _online_block(s, vt, m_ref, l_ref, acc_ref):
    m_old = m_ref[...]
    m_new = jnp.maximum(m_old, jnp.max(s, axis=1, keepdims=True))
    alpha = jnp.exp(m_old - m_new)
    p = jnp.exp(s - m_new)
    l_ref[...] = alpha * l_ref[...] + jnp.sum(p, axis=1, keepdims=True)
    acc_ref[...] = alpha * acc_ref[...] + _nt_dot(p.astype(bf16), vt)
    m_ref[...] = m_new


def _finish_heads(o_ref, l_ref, acc_ref, T):
    R, W = acc_ref.shape
    rh = lax.broadcasted_iota(jnp.int32, (R, W), 0) // T
    ch = lax.broadcasted_iota(jnp.int32, (R, W), 1) // HEAD_DIM
    o = jnp.where(rh == ch, acc_ref[...] / l_ref[...], 0.0)
    o_ref[0] = jnp.sum(o.reshape(N_HEADS, T, W), axis=0).astype(bf16)


def _init_online(m_ref, l_ref, acc_ref):
    m_ref[...] = jnp.full(m_ref.shape, NEG, f32)
    l_ref[...] = jnp.zeros(l_ref.shape, f32)
    acc_ref[...] = jnp.zeros(acc_ref.shape, f32)


def _sample_a_kernel(n_steps, pt_ref, q_ref, bias_ref, kvn_ref, *rest):
    G = PAGES_PER_STEP
    pages = rest[:G]
    o_ref, m_ref, l_ref, acc_ref = rest[G:]
    j = pl.program_id(1)
    T = bias_ref.shape[1]
    W = W_HEADS
    q = q_ref[0]

    @pl.when(j == 0)
    def _():
        _init_online(m_ref, l_ref, acc_ref)

    def block(kv_ref, bias):
        s = _dot(q, kv_ref[0, 0:W, :].astype(bf16)) + jnp.tile(bias, (N_HEADS, 1))
        _online_block(s, kv_ref[0, W:2 * W, :].astype(bf16), m_ref, l_ref, acc_ref)

    for g in range(G):
        off = pl.multiple_of((j * G + g) * PAGE_SIZE, PAGE_SIZE)
        block(pages[g], bias_ref[0, :, pl.ds(off, PAGE_SIZE)])

    @pl.when(j == n_steps - 1)
    def _():
        past = n_steps * G * PAGE_SIZE
        block(kvn_ref, bias_ref[0, :, past:past + PAGE_SIZE])
        _finish_heads(o_ref, l_ref, acc_ref, T)


def _sample_a(page_table, q_bd, bias, kv_new, pool_kvt, T):
    Bd, n_pages = page_table.shape
    G = PAGES_PER_STEP
    n_steps = n_pages // G
    W = W_HEADS
    R = N_HEADS * T
    L = bias.shape[2]
    grid_spec = pltpu.PrefetchScalarGridSpec(
        num_scalar_prefetch=1,
        grid=(Bd, n_steps),
        in_specs=[pl.BlockSpec((1, R, W), lambda b, j, pt: (b, 0, 0)),
                  pl.BlockSpec((1, T, L), lambda b, j, pt: (b, 0, 0)),
                  pl.BlockSpec((1, 2 * W, PAGE_SIZE), lambda b, j, pt: (b, 0, 0))]
        + _page_specs((1, 2 * W, PAGE_SIZE), n_pages, n_steps, False),
        out_specs=pl.BlockSpec((1, T, W), lambda b, j, pt: (b, 0, 0)),
        scratch_shapes=[pltpu.VMEM((R, 1), f32), pltpu.VMEM((R, 1), f32), pltpu.VMEM((R, W), f32)],
    )
    return pl.pallas_call(
        functools.partial(_sample_a_kernel, n_steps),
        grid_spec=grid_spec,
        out_shape=jax.ShapeDtypeStruct((Bd, T, W), bf16),
        compiler_params=pltpu.CompilerParams(dimension_semantics=("arbitrary", "arbitrary"),
                                             vmem_limit_bytes=VMEM_LIMIT),
        name="sample_a",
    )(page_table.reshape(-1), q_bd, bias, kv_new, *([pool_kvt] * G))


def _sample_b_kernel(n_steps, pt_ref, q_ref, kvn_ref, lfn_ref, *rest):
    G = PAGES_PER_STEP
    kv_pages = rest[:G]
    lf_pages = rest[G:2 * G]
    o_ref, m_ref, l_ref, acc_ref, carry_ref = rest[2 * G:]
    j = pl.program_id(1)
    R = q_ref.shape[1]
    T = R // N_HEADS
    W = W_HEADS
    q = q_ref[0]
    u = _strict_lower_ones(PAGE_SIZE)

    def block(kv_ref, logf_row, mask):
        g, carry = _suffix_bias(logf_row, carry_ref[...], u)
        carry_ref[...] = carry
        bias = jnp.concatenate([jnp.broadcast_to(g[h:h + 1, :], (T, PAGE_SIZE)) for h in range(N_HEADS)], axis=0)
        s = _dot(q, kv_ref[0, 0:W, :].astype(bf16)) + bias
        if mask is not None:
            s = jnp.where(mask, s, NEG)
        _online_block(s, kv_ref[0, W:2 * W, :].astype(bf16), m_ref, l_ref, acc_ref)

    @pl.when(j == 0)
    def _():
        _init_online(m_ref, l_ref, acc_ref)
        carry_ref[...] = jnp.zeros(carry_ref.shape, f32)
        t = lax.broadcasted_iota(jnp.int32, (R, PAGE_SIZE), 0) % T
        jn = lax.broadcasted_iota(jnp.int32, (R, PAGE_SIZE), 1)
        block(kvn_ref, lfn_ref[0], jn <= t)

    for g in reversed(range(G)):
        block(kv_pages[g], lf_pages[g][0], None)

    @pl.when(j == n_steps - 1)
    def _():
        _finish_heads(o_ref, l_ref, acc_ref, T)


def _sample_b(page_table, q_bd, kv_new, logf_new, pool_kvt, pool_logft, T):
    Bd, n_pages = page_table.shape
    G = PAGES_PER_STEP
    n_steps = n_pages // G
    W = W_HEADS
    R = N_HEADS * T
    grid_spec = pltpu.PrefetchScalarGridSpec(
        num_scalar_prefetch=1,
        grid=(Bd, n_steps),
        in_specs=[pl.BlockSpec((1, R, W), lambda b, j, pt: (b, 0, 0)),
                  pl.BlockSpec((1, 2 * W, PAGE_SIZE), lambda b, j, pt: (b, 0, 0)),
                  pl.BlockSpec((1, N_HEADS, PAGE_SIZE), lambda b, j, pt: (b, 0, 0))]
        + _page_specs((1, 2 * W, PAGE_SIZE), n_pages, n_steps, True)
        + _page_specs((1, N_HEADS, PAGE_SIZE), n_pages, n_steps, True),
        out_specs=pl.BlockSpec((1, T, W), lambda b, j, pt: (b, 0, 0)),
        scratch_shapes=[pltpu.VMEM((R, 1), f32), pltpu.VMEM((R, 1), f32), pltpu.VMEM((R, W), f32),
                        pltpu.VMEM((N_HEADS, 1), f32)],
    )
    return pl.pallas_call(
        functools.partial(_sample_b_kernel, n_steps),
        grid_spec=grid_spec,
        out_shape=jax.ShapeDtypeStruct((Bd, T, W), bf16),
        compiler_params=pltpu.CompilerParams(dimension_semantics=("arbitrary", "arbitrary"),
                                             vmem_limit_bytes=VMEM_LIMIT),
        name="sample_b",
    )(page_table.reshape(-1), q_bd, kv_new, logf_new, *([pool_kvt] * G), *([pool_logft] * G))


def _rope_tables(pos):
    half = HEAD_DIM // 2
    inv = jnp.power(jnp.float32(ROPE_THETA), -jnp.arange(half, dtype=f32) / half)
    ang = pos.astype(f32)[:, None] * inv[None, :]
    cos = jnp.cos(ang)
    sin = jnp.sin(ang)
    cos64 = jnp.concatenate([cos, cos], axis=1)
    sin64 = jnp.concatenate([-sin, sin], axis=1)
    return jnp.tile(cos64, (1, N_HEADS)), jnp.tile(sin64, (1, N_HEADS)), cos.T, sin.T


def _layer_weights(l, norm1, w_in, b_f, qn_a, kn_a, qn_b, kn_b, w_up_a, w_up_b, w_o, norm2, w_ffn_in, w_ffn_out):
    D = w_in.shape[1]
    W = W_HEADS
    w_t = w_in[l].T
    splits = (W, W, W, W, D_IDX, N_HEADS, W, W, W, N_HEADS, D, D)
    cuts = [int(c) for c in np.cumsum(splits)[:-1]]
    q_a, k_a, v_a, q_i, k_i, w_i, q_b, k_b, v_b, f_b, g_a, g_b = jnp.split(w_t, cuts, axis=0)
    pad_rows = lambda n: jnp.zeros((n, D), w_t.dtype)
    w_row = jnp.concatenate([q_a, q_i, q_b, w_i, pad_rows(LANES - N_HEADS), g_a, g_b], axis=0).T.astype(bf16)
    w_feat = jnp.concatenate([k_a, v_a, k_b, v_b, k_i, f_b, pad_rows(SMALL_ROWS - N_HEADS)], axis=0).astype(bf16)
    d_ff = w_ffn_out.shape[1]
    tile_g = lambda g: jnp.tile(g[l].astype(f32), N_HEADS)[None, :]
    col_g = lambda g: jnp.broadcast_to(g[l].astype(f32)[:, None], (HEAD_DIM, ROW_TILE))
    r = np.arange(W) // HEAD_DIM
    bd = jnp.asarray((r[:, None] == r[None, :]).astype(np.float32)).astype(bf16)
    return {
        "n1": norm1[l][None, :], "w_row": w_row, "w_t": w_feat, "bf_col": b_f[l][:, None],
        "gqa": tile_g(qn_a), "gka": col_g(kn_a), "gqb": tile_g(qn_b), "gkb": col_g(kn_b), "bd": bd,
        "w_up_a": w_up_a[l].astype(bf16), "w_up_b": w_up_b[l].astype(bf16), "w_o": w_o[l].astype(bf16),
        "n2": norm2[l][None, :], "w_gate": w_ffn_in[l][:, :d_ff].astype(bf16),
        "w_upf": w_ffn_in[l][:, d_ff:].astype(bf16), "w_out": w_ffn_out[l].astype(bf16),
    }


def _rows_by_head(x_hm, Bd, T):
    return jnp.transpose(x_hm.reshape(N_HEADS, Bd, T, HEAD_DIM), (1, 0, 2, 3))


def _block_diag_q(x_hm, Bd, T):
    q = _rows_by_head(x_hm, Bd, T)
    eye = jnp.eye(N_HEADS, dtype=q.dtype)
    return jnp.einsum("bhtd,hg->bhtgd", q, eye).reshape(Bd, N_HEADS * T, W_HEADS)


def _new_token_block(x_t, Bd, T):
    r = x_t.shape[1]
    x = jnp.transpose(x_t[0].reshape(r, Bd, T), (1, 0, 2))
    return jnp.pad(x, ((0, 0), (0, 0), (0, PAGE_SIZE - T)))


def _token_major(x_t, Bd, T, feat_shape):
    r = x_t.shape[1]
    return jnp.transpose(x_t[0].reshape(r, Bd, T), (1, 2, 0)).reshape((Bd, T) + feat_shape)


def _slots_last(pool):
    n_pool = pool.shape[0]
    return jnp.moveaxis(pool, 1, -1).reshape(n_pool, -1, PAGE_SIZE)


def kernel(x_prompt, x_sample, cache_a_kv, cache_idx_k, cache_b_kv, cache_b_logf, page_table, norm1, w_in, b_f,
           qn_a, kn_a, qn_b, kn_b, w_up_a, w_up_b, w_o, norm2, w_ffn_in, w_ffn_out):
    B, S, D = x_prompt.shape
    Bd, T, _ = x_sample.shape
    depth = w_in.shape[0]
    n_pages = page_table.shape[1]
    past = n_pages * PAGE_SIZE
    W = W_HEADS
    kv_shape = (2, N_HEADS, HEAD_DIM)
    assert S % ROW_TILE == 0 and Bd * T == ROW_TILE and S % Q_BLOCK == 0
    assert n_pages % PAGES_PER_STEP == 0 and T <= 8

    tab_p = _rope_tables(jnp.arange(S))
    tab_s = _rope_tables(jnp.tile(past + jnp.arange(T), Bd))

    h_p = x_prompt.reshape(B * S, D)
    h_s = x_sample.reshape(Bd * T, D)
    outs = [[] for _ in range(8)]
    for l in range(depth):
        wts = _layer_weights(l, norm1, w_in, b_f, qn_a, kn_a, qn_b, kn_b, w_up_a, w_up_b, w_o, norm2, w_ffn_in,
                             w_ffn_out)
        p = _proj(h_p, S, tab_p, wts)
        o_a = _mixer_a_prompt(p, B, S)
        o_b = _mixer_b_prompt(p, B, S)
        h_p = _merge_ffn(h_p, o_a, o_b, p["ga"], p["gb"], wts)
        outs[0].append(jnp.moveaxis(p["akv"].reshape((B,) + kv_shape + (S,)), -1, 1))
        outs[1].append(jnp.moveaxis(p["idxk"], -1, 1))
        outs[2].append(jnp.moveaxis(p["bkv"].reshape((B,) + kv_shape + (S,)), -1, 1))
        outs[3].append(jnp.moveaxis(p["logf"], -1, 1))

        s = _proj(h_s, Bd * T, tab_s, wts)
        qi_rows = _rows_by_head(s["qi"], Bd, T).reshape(Bd, N_HEADS * T, D_IDX)
        w_col = jnp.transpose(s["aux"][:, 0:N_HEADS].reshape(Bd, T, N_HEADS), (0, 2, 1))
        w_col = w_col.reshape(Bd, N_HEADS * T, 1)
        bias = _sample_index(page_table, qi_rows, w_col, _new_token_block(s["idxk"], Bd, T),
                             _slots_last(cache_idx_k[l]), T)
        o_a = _sample_a(page_table, _block_diag_q(s["qa"], Bd, T), bias, _new_token_block(s["akv"], Bd, T),
                        _slots_last(cache_a_kv[l]), T)
        o_b = _sample_b(page_table, _block_diag_q(s["qb"], Bd, T), _new_token_block(s["bkv"], Bd, T),
                        _new_token_block(s["logf"], Bd, T), _slots_last(cache_b_kv[l]),
                        _slots_last(cache_b_logf[l]), T)
        h_s = _merge_ffn(h_s, o_a.reshape(Bd * T, W), o_b.reshape(Bd * T, W), s["ga"], s["gb"], wts)
        outs[4].append(_token_major(s["akv"], Bd, T, kv_shape))
        outs[5].append(_token_major(s["idxk"], Bd, T, (D_IDX,)))
        outs[6].append(_token_major(s["bkv"], Bd, T, kv_shape))
        outs[7].append(_token_major(s["logf"], Bd, T, (N_HEADS,)))

    return (h_p.reshape(B, S, D), h_s.reshape(Bd, T, D)) + tuple(jnp.stack(o) for o in outs)
```

```python
import functools

import jax
import jax.numpy as jnp
import numpy as np
from jax import lax
from jax.experimental import pallas as pl
from jax.experimental.pallas import tpu as pltpu

HEAD_DIM = 64
N_HEADS = 8
D_IDX = 64
PAGE_SIZE = 128
TOPK_MAX = 256
ROPE_THETA = 10000.0
EPS = 1e-6
ATTN_SCALE = HEAD_DIM ** -0.5
IDX_SCALE = (N_HEADS * D_IDX) ** -0.5
Q_BLOCK = 128
KEY_CHUNK = 512
ROW_TILE = 256
LANES = 128
SMALL_ROWS = 16
NEG = -1e30
VMEM_LIMIT = 56 * 1024 * 1024
INT_MIN = -2 ** 31
KV_PAGES_PER_STEP = 16
IDX_PAGES_PER_STEP = 32

bf16 = jnp.bfloat16
f32 = jnp.float32
W_HEADS = N_HEADS * HEAD_DIM


def _nt_dot(a, b):
    return lax.dot_general(a, b, (((1,), (1,)), ((), ())), preferred_element_type=f32)


def _dot(a, b):
    return jnp.dot(a, b, preferred_element_type=f32)


def _split3(x):
    a1 = x.astype(bf16)
    r1 = x - a1.astype(f32)
    a2 = r1.astype(bf16)
    a3 = (r1 - a2.astype(f32)).astype(bf16)
    return a1, a2, a3


def _log_sigmoid(x):
    return -(jnp.maximum(-x, 0.0) + jnp.log1p(jnp.exp(-jnp.abs(x))))


def _sigmoid(x):
    return 1.0 / (1.0 + jnp.exp(-x))


def _rmsnorm_rows(x, g):
    return x * lax.rsqrt(jnp.mean(x * x, axis=-1, keepdims=True) + EPS) * g


def _head_rmsnorm(x, g, bd):
    x2 = x * x
    hi = x2.astype(bf16)
    lo = (x2 - hi.astype(f32)).astype(bf16)
    ss = _dot(hi, bd) + _dot(lo, bd)
    return x * lax.rsqrt(ss * (1.0 / HEAD_DIM) + EPS) * g


def _rope(x, cos, sin_signed):
    w = x.shape[-1]
    half = HEAD_DIM // 2
    lane = lax.broadcasted_iota(jnp.int32, x.shape, 1)
    first = (lane & half) == 0
    partner = jnp.where(first, pltpu.roll(x, w - half, 1), pltpu.roll(x, half, 1))
    return x * cos + partner * sin_signed


def _head_rmsnorm_t(x, g):
    ms = jnp.mean(x * x, axis=1, keepdims=True)
    return x * lax.rsqrt(ms + EPS) * g[None]


def _rope_t(x, cos, sin):
    half = HEAD_DIM // 2
    x1 = x[:, :half, :]
    x2 = x[:, half:, :]
    return jnp.concatenate([x1 * cos[None] - x2 * sin[None], x2 * cos[None] + x1 * sin[None]], axis=1)


def _proj_kernel(x_ref, n1_ref, wr_ref, wt_ref, bfc_ref, gqa_ref, gka_ref, gqb_ref, gkb_ref,
                 cos_ref, sin_ref, cost_ref, sint_ref, bd_ref,
                 akv_ref, idxk_ref, bkv_ref, logf_ref,
                 qa_ref, qi_ref, qb_ref, aux_ref, ga_ref, gb_ref,
                 kat_ref, vat_ref, kbt_ref, vbt_ref, kit_ref):
    W = W_HEADS
    xn = _rmsnorm_rows(x_ref[...], n1_ref[...])
    xb = xn.astype(bf16)
    bd = bd_ref[...]
    cos = cos_ref[...]
    sin = sin_ref[...]
    cos_t = cost_ref[...]
    sin_t = sint_ref[...]
    n = xb.shape[0]

    def heads_out(ref, val):
        vb = val.astype(bf16)
        for h in range(N_HEADS):
            ref[h] = vb[:, h * HEAD_DIM:(h + 1) * HEAD_DIM]

    q_a = _dot(xb, wr_ref[:, 0:W])
    heads_out(qa_ref, _rope(_head_rmsnorm(q_a, gqa_ref[...], bd), cos, sin) * ATTN_SCALE)
    heads_out(qi_ref, _rope(_dot(xb, wr_ref[:, W:2 * W]), cos, sin))
    q_b = _dot(xb, wr_ref[:, 2 * W:3 * W])
    heads_out(qb_ref, _head_rmsnorm(q_b, gqb_ref[...], bd) * ATTN_SCALE)
    c0 = 3 * W
    aux_ref[...] = _dot(xb, wr_ref[:, c0:c0 + LANES])
    c0 += LANES
    D = ga_ref.shape[1]
    ga_ref[...] = _dot(xb, wr_ref[:, c0:c0 + D])
    gb_ref[...] = _dot(xb, wr_ref[:, c0 + D:c0 + 2 * D])

    def t_proj(r0, rows):
        return _nt_dot(wt_ref[r0:r0 + rows, :], xb)

    k_a = _rope_t(_head_rmsnorm_t(t_proj(0, W).reshape(N_HEADS, HEAD_DIM, n), gka_ref[...]), cos_t, sin_t)
    k_a = k_a.reshape(W, n)
    akv_ref[0, 0:W, :] = k_a
    kat_ref[...] = k_a.astype(bf16)
    v_a = t_proj(W, W)
    akv_ref[0, W:2 * W, :] = v_a
    vat_ref[...] = v_a.astype(bf16)
    k_b = _head_rmsnorm_t(t_proj(2 * W, W).reshape(N_HEADS, HEAD_DIM, n), gkb_ref[...]).reshape(W, n)
    bkv_ref[0, 0:W, :] = k_b
    kbt_ref[...] = k_b.astype(bf16)
    v_b = t_proj(3 * W, W)
    bkv_ref[0, W:2 * W, :] = v_b
    vbt_ref[...] = v_b.astype(bf16)
    k_i = _rope_t(t_proj(4 * W, D_IDX)[None], cos_t, sin_t)[0]
    idxk_ref[0] = k_i
    kit_ref[...] = k_i.astype(bf16)
    f = t_proj(4 * W + D_IDX, SMALL_ROWS)[0:N_HEADS, :]
    logf_ref[0] = _log_sigmoid(f + bfc_ref[...])


def _const_spec(shape):
    nd = len(shape)
    return pl.BlockSpec(shape, lambda *_: (0,) * nd, pipeline_mode=pl.Buffered(1))


def _proj(x, rows_per_batch, tables, wts):
    N, D = x.shape
    W = W_HEADS
    tm = ROW_TILE
    rpb = rows_per_batch
    nb = N // rpb
    nt = rpb // tm
    cos, sin, cos_t, sin_t = tables
    row = lambda w: pl.BlockSpec((tm, w), lambda i: (i, 0))
    tab = lambda w: pl.BlockSpec((tm, w), lambda i: (i % nt, 0))
    tab_t = pl.BlockSpec((HEAD_DIM // 2, tm), lambda i: (0, i % nt))
    hm = pl.BlockSpec((N_HEADS, tm, HEAD_DIM), lambda i: (0, i, 0))
    fm = lambda r: pl.BlockSpec((1, r, tm), lambda i: (i // nt, 0, i % nt))
    fm2 = lambda r: pl.BlockSpec((r, tm), lambda i: (0, i))
    consts = [wts["n1"], wts["w_row"], wts["w_t"], wts["bf_col"], wts["gqa"], wts["gka"], wts["gqb"], wts["gkb"]]
    in_specs = ([row(D)] + [_const_spec(c.shape) for c in consts]
                + [tab(W), tab(W), tab_t, tab_t, _const_spec(wts["bd"].shape)])
    sds = jax.ShapeDtypeStruct
    out_shape = [sds((nb, 2 * W, rpb), f32), sds((nb, D_IDX, rpb), f32), sds((nb, 2 * W, rpb), f32),
                 sds((nb, N_HEADS, rpb), f32)]
    out_specs = [fm(2 * W), fm(D_IDX), fm(2 * W), fm(N_HEADS)]
    out_shape += [sds((N_HEADS, N, HEAD_DIM), bf16)] * 3 + [sds((N, LANES), f32), sds((N, D), f32), sds((N, D), f32)]
    out_specs += [hm] * 3 + [row(LANES), row(D), row(D)]
    out_shape += [sds((W, N), bf16)] * 4 + [sds((D_IDX, N), bf16)]
    out_specs += [fm2(W)] * 4 + [fm2(D_IDX)]
    names = ["akv", "idxk", "bkv", "logf", "qa", "qi", "qb", "aux", "ga", "gb", "kat", "vat", "kbt", "vbt", "kit"]
    outs = pl.pallas_call(
        _proj_kernel,
        grid=(N // tm,),
        in_specs=in_specs,
        out_specs=out_specs,
        out_shape=out_shape,
        compiler_params=pltpu.CompilerParams(dimension_semantics=("arbitrary",), vmem_limit_bytes=VMEM_LIMIT),
        name="proj",
    )(x, *consts, cos, sin, cos_t, sin_t, wts["bd"])
    return dict(zip(names, outs))


def _ordinal_to_float(o):
    bits = jnp.where(o < 0, o ^ jnp.int32(0x7FFFFFFF), o)
    return lax.bitcast_convert_type(bits, f32)


def _count(mask):
    ones = jnp.where(mask, 1.0, 0.0)
    n = ones.shape[1]
    parts = [ones[:, c:min(c + LANES, n)] for c in range(0, n, LANES)]
    while len(parts) > 1:
        parts = [parts[a] + parts[a + 1] for a in range(0, len(parts) - 1, 2)] + (parts[-1:] if len(parts) % 2 else [])
    return jnp.sum(parts[0], axis=1, keepdims=True)


def _topk_mask(score_ref, k, idx_bits):
    R, L = score_ref.shape
    kf = jnp.float32(k)

    zero = jnp.zeros((R, 1), jnp.int32)
    t0 = jnp.where(_count(score_ref[...] >= 0.0) >= kf, zero, jnp.full((R, 1), INT_MIN, jnp.int32))

    def vbody(i, t):
        cand = t | jnp.left_shift(jnp.int32(1), 30 - i)
        return jnp.where(_count(score_ref[...] >= _ordinal_to_float(cand)) >= kf, cand, t)

    t_ord = lax.fori_loop(0, 31, vbody, t0, unroll=R > 8)
    tiny = jnp.float32(np.finfo(np.float32).tiny)
    flush = lambda x: jnp.where(jnp.abs(x) < tiny, 0.0, x)
    thr = flush(_ordinal_to_float(t_ord))
    nxt = jnp.where(thr == 0.0, tiny, flush(_ordinal_to_float(t_ord + 1)))

    score = score_ref[...]
    take_all = _count(score > -jnp.inf) <= kf
    n_above = _count(score >= nxt)
    n_ties = _count(score >= thr) - n_above
    lane = lax.broadcasted_iota(jnp.int32, (R, L), 1)

    def search():
        def vref(i, c):
            lo, hi = c
            mid = 0.5 * lo + 0.5 * hi
            ok = _count(score_ref[...] >= mid) >= kf
            return jnp.where(ok, mid, lo), jnp.where(ok, hi, mid)

        lo, hi = lax.fori_loop(0, 24, vref, (thr, nxt))
        need = kf - _count(score_ref[...] >= hi)

        def ibody(i, c):
            a, b = c
            mid = (a + b) >> 1
            s = score_ref[...]
            ok = _count((s >= lo) & (s < hi) & (lane <= mid)) >= need
            return jnp.where(ok, a, mid), jnp.where(ok, mid, b)

        a0 = jnp.full((R, 1), -1, jnp.int32)
        b0 = jnp.full((R, 1), L - 1, jnp.int32)
        return lo, hi, lax.fori_loop(0, idx_bits, ibody, (a0, b0))[1]

    straddle = jnp.where(take_all, -1.0, n_ties - (kf - n_above))
    lo, hi, cut = lax.cond(jnp.max(straddle) > 0.0, search, lambda: (thr, nxt, jnp.full((R, 1), L - 1, jnp.int32)))
    score = score_ref[...]
    return (score >= hi) | ((score >= lo) & (score < hi) & (lane <= cut)) | take_all


def _key_classes(S, tq):
    ch = min(KEY_CHUNK, S)
    return list(range(ch, S + 1, ch))


def _for_key_class(i, tq, classes, body):
    need = (i + 1) * tq
    lo = 0
    for n in classes:
        pl.when((need > lo) & (need <= n))(functools.partial(body, n))
        lo = n


def _softmax_pv_t(s, vt):
    m = jnp.max(s, axis=1, keepdims=True)
    p = jnp.exp(s - m)
    l = jnp.sum(p, axis=1, keepdims=True)
    return _nt_dot(p.astype(bf16), vt) / l


def _mixer_a_prompt_kernel(topk, idx_bits, classes, qi_ref, aux_ref, kit_ref, qa_ref, kat_ref, vat_ref, o_ref,
                           score_ref, bias_ref):
    i = pl.program_id(1)
    tq = qi_ref.shape[1]

    def body(n):
        kit = kit_ref[:, 0:n]
        aux = aux_ref[...]
        score = jnp.zeros((tq, n), f32)
        for h in range(N_HEADS):
            dots = _dot(qi_ref[h], kit)
            w_h = aux[:, h:h + 1] * IDX_SCALE
            score = score + jnp.maximum(dots, 0.0) * w_h
        qpos = i * tq + lax.broadcasted_iota(jnp.int32, (tq, n), 0)
        kpos = lax.broadcasted_iota(jnp.int32, (tq, n), 1)
        causal = kpos <= qpos
        masked = score_ref.at[:, 0:n]
        masked[...] = jnp.where(causal, score, -jnp.inf)
        sel = _topk_mask(masked, topk, idx_bits)
        bias_ref[:, 0:n] = jnp.where(sel & causal, 0.0, -jnp.inf)
        outs = []
        for h in range(N_HEADS):
            r = slice(h * HEAD_DIM, (h + 1) * HEAD_DIM)
            s = _dot(qa_ref[h], kat_ref[r, 0:n]) + bias_ref[:, 0:n]
            outs.append(_softmax_pv_t(s, vat_ref[r, 0:n]))
        o_ref[...] = jnp.concatenate(outs, axis=1).astype(bf16)

    _for_key_class(i, tq, classes, body)


def _mixer_a_prompt(p, B, S):
    N = B * S
    W = W_HEADS
    tq = Q_BLOCK
    nq = S // tq
    topk = min(TOPK_MAX, S // 4)
    idx_bits = int(np.ceil(np.log2(S)))
    qblk = pl.BlockSpec((N_HEADS, tq, HEAD_DIM), lambda b, i: (0, b * nq + i, 0))
    seq = lambda r: pl.BlockSpec((r, S), lambda b, i: (0, b))
    return pl.pallas_call(
        functools.partial(_mixer_a_prompt_kernel, topk, idx_bits, _key_classes(S, tq)),
        grid=(B, nq),
        in_specs=[qblk, pl.BlockSpec((tq, LANES), lambda b, i: (b * nq + i, 0)), seq(D_IDX), qblk, seq(W), seq(W)],
        out_specs=pl.BlockSpec((tq, W), lambda b, i: (b * nq + i, 0)),
        out_shape=jax.ShapeDtypeStruct((N, W), bf16),
        scratch_shapes=[pltpu.VMEM((tq, S), f32), pltpu.VMEM((tq, S), f32)],
        compiler_params=pltpu.CompilerParams(dimension_semantics=("arbitrary", "arbitrary"),
                                             vmem_limit_bytes=VMEM_LIMIT),
        name="mixer_a_prompt",
    )(p["qi"], p["aux"], p["kit"], p["qa"], p["kat"], p["vat"])


def _suffix_bias(logf_row, carry, strict_upper):
    a1, a2, a3 = _split3(logf_row)
    inner = _dot(a1, strict_upper) + _dot(a2, strict_upper) + _dot(a3, strict_upper)
    return inner + carry, carry + jnp.sum(logf_row, axis=1, keepdims=True)


def _strict_lower_ones(n):
    r = lax.broadcasted_iota(jnp.int32, (n, n), 0)
    c = lax.broadcasted_iota(jnp.int32, (n, n), 1)
    return jnp.where(r > c, 1.0, 0.0).astype(bf16)


def _mixer_b_prompt_kernel(classes, logf_ref, qb_ref, kbt_ref, vbt_ref, o_ref, g_ref):
    i = pl.program_id(1)
    tq = qb_ref.shape[1]
    S = kbt_ref.shape[1]

    @pl.when(i == 0)
    def _():
        u = _strict_lower_ones(LANES)
        carry = jnp.zeros((N_HEADS, 1), f32)
        for c in reversed(range(S // LANES)):
            g, carry = _suffix_bias(logf_ref[0, :, c * LANES:(c + 1) * LANES], carry, u)
            g_ref[:, c * LANES:(c + 1) * LANES] = g

    def body(n):
        qpos = i * tq + lax.broadcasted_iota(jnp.int32, (tq, n), 0)
        kpos = lax.broadcasted_iota(jnp.int32, (tq, n), 1)
        causal = kpos <= qpos
        outs = []
        for h in range(N_HEADS):
            r = slice(h * HEAD_DIM, (h + 1) * HEAD_DIM)
            s = _dot(qb_ref[h], kbt_ref[r, 0:n]) + g_ref[h:h + 1, 0:n]
            s = jnp.where(causal, s, -jnp.inf)
            outs.append(_softmax_pv_t(s, vbt_ref[r, 0:n]))
        o_ref[...] = jnp.concatenate(outs, axis=1).astype(bf16)

    _for_key_class(i, tq, classes, body)


def _mixer_b_prompt(p, B, S):
    N = B * S
    W = W_HEADS
    tq = Q_BLOCK
    nq = S // tq
    qblk = pl.BlockSpec((N_HEADS, tq, HEAD_DIM), lambda b, i: (0, b * nq + i, 0))
    seq = pl.BlockSpec((W, S), lambda b, i: (0, b))
    return pl.pallas_call(
        functools.partial(_mixer_b_prompt_kernel, _key_classes(S, tq)),
        grid=(B, nq),
        in_specs=[pl.BlockSpec((1, N_HEADS, S), lambda b, i: (b, 0, 0)), qblk, seq, seq],
        out_specs=pl.BlockSpec((tq, W), lambda b, i: (b * nq + i, 0)),
        out_shape=jax.ShapeDtypeStruct((N, W), bf16),
        scratch_shapes=[pltpu.VMEM((N_HEADS, S), f32)],
        compiler_params=pltpu.CompilerParams(dimension_semantics=("arbitrary", "arbitrary"),
                                             vmem_limit_bytes=VMEM_LIMIT),
        name="mixer_b_prompt",
    )(p["logf"], p["qb"], p["kbt"], p["vbt"])


def _merge_ffn_kernel(n_ff_chunks, x_ref, oa_ref, ob_ref, ga_ref, gb_ref, wua_ref, wub_ref, wo_ref, n2_ref,
                      wg_ref, wu_ref, wout_ref, y_ref):
    u_a = _dot(oa_ref[...], wua_ref[...])
    u_b = _dot(ob_ref[...], wub_ref[...])
    m = _sigmoid(ga_ref[...]) * u_a + _sigmoid(gb_ref[...]) * u_b
    x1 = x_ref[...] + _dot(m.astype(bf16), wo_ref[...])
    hn = _rmsnorm_rows(x1, n2_ref[...]).astype(bf16)
    d_ff = wg_ref.shape[1]
    cw = d_ff // n_ff_chunks
    acc = x1
    for c in range(n_ff_chunks):
        gate = _dot(hn, wg_ref[:, c * cw:(c + 1) * cw])
        up = _dot(hn, wu_ref[:, c * cw:(c + 1) * cw])
        act = gate * _sigmoid(gate) * up
        acc = acc + _dot(act.astype(bf16), wout_ref[c * cw:(c + 1) * cw, :])
    y_ref[...] = acc


def _merge_ffn(x, o_a, o_b, g_a, g_b, wts):
    N, D = x.shape
    W = W_HEADS
    tm = ROW_TILE
    row = lambda w: pl.BlockSpec((tm, w), lambda i: (i, 0))
    consts = [wts["w_up_a"], wts["w_up_b"], wts["w_o"], wts["n2"], wts["w_gate"], wts["w_upf"], wts["w_out"]]
    d_ff = wts["w_gate"].shape[1]
    n_chunks = 2 if d_ff % (2 * LANES) == 0 else 1
    return pl.pallas_call(
        functools.partial(_merge_ffn_kernel, n_chunks),
        grid=(N // tm,),
        in_specs=[row(D), row(W), row(W), row(D), row(D)] + [_const_spec(c.shape) for c in consts],
        out_specs=row(D),
        out_shape=jax.ShapeDtypeStruct((N, D), f32),
        compiler_params=pltpu.CompilerParams(dimension_semantics=("arbitrary",), vmem_limit_bytes=VMEM_LIMIT),
        name="merge_ffn",
    )(x, o_a, o_b, g_a, g_b, *consts)


def _pages_per_step(n_pages, want):
    g = min(want, n_pages)
    assert n_pages % g == 0
    return g


def _page_specs(block, n_pages, G, reverse):
    n_steps = n_pages // G
    specs = []
    for g in range(G):
        if reverse:
            fn = lambda b, j, pt, g=g: (pt[b * n_pages + (n_steps - 1 - j) * G + g], 0, 0)
        else:
            fn = lambda b, j, pt, g=g: (pt[b * n_pages + j * G + g], 0, 0)
        specs.append(pl.BlockSpec(block, fn))
    return specs


def _sample_index_kernel(topk, idx_bits, n_steps, G, pt_ref, qi_ref, w_ref, kin_ref, *rest):
    pages = rest[:G]
    bias_ref, score_ref = rest[G:]
    j = pl.program_id(1)
    T = score_ref.shape[0]
    qi = qi_ref[0]
    w = w_ref[0] * IDX_SCALE

    def page_scores(kits):
        dots = jnp.concatenate([_dot(qi, k.astype(bf16)) for k in kits], axis=1)
        contrib = jnp.maximum(dots, 0.0) * w
        return jnp.sum(contrib.reshape(N_HEADS, T, dots.shape[1]), axis=0)

    off = pl.multiple_of(j * (G * PAGE_SIZE), G * PAGE_SIZE)
    score_ref[:, pl.ds(off, G * PAGE_SIZE)] = page_scores([pages[g][0] for g in range(G)])

    @pl.when(j == n_steps - 1)
    def _():
        past = n_steps * G * PAGE_SIZE
        L = score_ref.shape[1]
        sc = page_scores([kin_ref[0]])
        t = lax.broadcasted_iota(jnp.int32, (T, PAGE_SIZE), 0)
        jn = lax.broadcasted_iota(jnp.int32, (T, PAGE_SIZE), 1)
        score_ref[:, past:past + PAGE_SIZE] = jnp.where(jn <= t, sc, -jnp.inf)
        sel = _topk_mask(score_ref, topk, idx_bits)
        kpos = lax.broadcasted_iota(jnp.int32, (T, L), 1)
        qpos = past + lax.broadcasted_iota(jnp.int32, (T, L), 0)
        bias_ref[0] = jnp.where(sel & (kpos <= qpos), 0.0, NEG)


def _sample_index(page_table, qi_rows, w_col, ki_new, pool_kit, T):
    Bd, n_pages = page_table.shape
    G = _pages_per_step(n_pages, IDX_PAGES_PER_STEP)
    n_steps = n_pages // G
    past = n_pages * PAGE_SIZE
    L = past + PAGE_SIZE
    topk = min(TOPK_MAX, (past + T) // 4)
    idx_bits = int(np.ceil(np.log2(L)))
    R = N_HEADS * T
    grid_spec = pltpu.PrefetchScalarGridSpec(
        num_scalar_prefetch=1,
        grid=(Bd, n_steps),
        in_specs=[pl.BlockSpec((1, R, D_IDX), lambda b, j, pt: (b, 0, 0)),
                  pl.BlockSpec((1, R, 1), lambda b, j, pt: (b, 0, 0)),
                  pl.BlockSpec((1, D_IDX, PAGE_SIZE), lambda b, j, pt: (b, 0, 0))]
        + _page_specs((1, D_IDX, PAGE_SIZE), n_pages, G, False),
        out_specs=pl.BlockSpec((1, T, L), lambda b, j, pt: (b, 0, 0)),
        scratch_shapes=[pltpu.VMEM((T, L), f32)],
    )
    return pl.pallas_call(
        functools.partial(_sample_index_kernel, topk, idx_bits, n_steps, G),
        grid_spec=grid_spec,
        out_shape=jax.ShapeDtypeStruct((Bd, T, L), f32),
        compiler_params=pltpu.CompilerParams(dimension_semantics=("arbitrary", "arbitrary"),
                                             vmem_limit_bytes=VMEM_LIMIT),
        name="sample_index",
    )(page_table.reshape(-1), qi_rows, w_col, ki_new, *([pool_kit] * G))


def _online_blocks(q, kv_refs, biases, m_ref, l_ref, acc_ref):
    W = W_HEADS
    s = jnp.concatenate([_dot(q, r[0, 0:W, :].astype(bf16)) + b for r, b in zip(kv_refs, biases)], axis=1)
    m_old = m_ref[...]
    m_new = jnp.maximum(m_old, jnp.max(s, axis=1, keepdims=True))
    alpha = jnp.exp(m_old - m_new)
    p = jnp.exp(s - m_new)
    l_ref[...] = alpha * l_ref[...] + jnp.sum(p, axis=1, keepdims=True)
    pb = p.astype(bf16)
    pv = None
    for g, r in enumerate(kv_refs):
        d = _nt_dot(pb[:, g * PAGE_SIZE:(g + 1) * PAGE_SIZE], r[0, W:2 * W, :].astype(bf16))
        pv = d if pv is None else pv + d
    acc_ref[...] = alpha * acc_ref[...] + pv
    m_ref[...] = m_new


def _finish_heads(o_ref, l_ref, acc_ref, T):
    R, W = acc_ref.shape
    rh = lax.broadcasted_iota(jnp.int32, (R, W), 0) // T
    ch = lax.broadcasted_iota(jnp.int32, (R, W), 1) // HEAD_DIM
    o = jnp.where(rh == ch, acc_ref[...] / l_ref[...], 0.0)
    o_ref[0] = jnp.sum(o.reshape(N_HEADS, T, W), axis=0).astype(bf16)


def _init_online(m_ref, l_ref, acc_ref):
    m_ref[...] = jnp.full(m_ref.shape, NEG, f32)
    l_ref[...] = jnp.zeros(l_ref.shape, f32)
    acc_ref[...] = jnp.zeros(acc_ref.shape, f32)


def _sample_a_kernel(n_steps, G, pt_ref, q_ref, bias_ref, kvn_ref, *rest):
    pages = rest[:G]
    o_ref, m_ref, l_ref, acc_ref = rest[G:]
    j = pl.program_id(1)
    T = bias_ref.shape[1]
    q = q_ref[0]

    @pl.when(j == 0)
    def _():
        _init_online(m_ref, l_ref, acc_ref)

    def update(with_new):
        off = pl.multiple_of(j * (G * PAGE_SIZE), G * PAGE_SIZE)
        b_all = bias_ref[0, :, pl.ds(off, G * PAGE_SIZE)]
        refs = list(pages)
        biases = [jnp.tile(b_all[:, g * PAGE_SIZE:(g + 1) * PAGE_SIZE], (N_HEADS, 1)) for g in range(G)]
        if with_new:
            past = n_steps * G * PAGE_SIZE
            refs.append(kvn_ref)
            biases.append(jnp.tile(bias_ref[0, :, past:past + PAGE_SIZE], (N_HEADS, 1)))
        _online_blocks(q, refs, biases, m_ref, l_ref, acc_ref)

    pl.when(j < n_steps - 1)(functools.partial(update, False))

    @pl.when(j == n_steps - 1)
    def _():
        update(True)
        _finish_heads(o_ref, l_ref, acc_ref, T)


def _sample_a(page_table, q_bd, bias, kv_new, pool_kvt, T):
    Bd, n_pages = page_table.shape
    G = _pages_per_step(n_pages, KV_PAGES_PER_STEP)
    n_steps = n_pages // G
    W = W_HEADS
    R = N_HEADS * T
    L = bias.shape[2]
    grid_spec = pltpu.PrefetchScalarGridSpec(
        num_scalar_prefetch=1,
        grid=(Bd, n_steps),
        in_specs=[pl.BlockSpec((1, R, W), lambda b, j, pt: (b, 0, 0)),
                  pl.BlockSpec((1, T, L), lambda b, j, pt: (b, 0, 0)),
                  pl.BlockSpec((1, 2 * W, PAGE_SIZE), lambda b, j, pt: (b, 0, 0))]
        + _page_specs((1, 2 * W, PAGE_SIZE), n_pages, G, False),
        out_specs=pl.BlockSpec((1, T, W), lambda b, j, pt: (b, 0, 0)),
        scratch_shapes=[pltpu.VMEM((R, 1), f32), pltpu.VMEM((R, 1), f32), pltpu.VMEM((R, W), f32)],
    )
    return pl.pallas_call(
        functools.partial(_sample_a_kernel, n_steps, G),
        grid_spec=grid_spec,
        out_shape=jax.ShapeDtypeStruct((Bd, T, W), bf16),
        compiler_params=pltpu.CompilerParams(dimension_semantics=("arbitrary", "arbitrary"),
                                             vmem_limit_bytes=VMEM_LIMIT),
        name="sample_a",
    )(page_table.reshape(-1), q_bd, bias, kv_new, *([pool_kvt] * G))


def _sample_b_kernel(n_steps, G, pt_ref, q_ref, kvn_ref, lfn_ref, *rest):
    kv_pages = rest[:G]
    lf_pages = rest[G:2 * G]
    o_ref, m_ref, l_ref, acc_ref, carry_ref = rest[2 * G:]
    j = pl.program_id(1)
    R = q_ref.shape[1]
    T = R // N_HEADS
    q = q_ref[0]

    def update(with_new):
        kv = [kv_pages[g] for g in reversed(range(G))]
        lf = [lf_pages[g][0] for g in reversed(range(G))]
        if with_new:
            kv = [kvn_ref] + kv
            lf = [lfn_ref[0]] + lf
        lf_all = jnp.concatenate(lf, axis=0)
        inner, _ = _suffix_bias(lf_all, 0.0, _strict_lower_ones(PAGE_SIZE))
        total = jnp.sum(lf_all, axis=1, keepdims=True)
        carry = carry_ref[...]
        biases = []
        for i in range(len(kv)):
            g = inner[i * N_HEADS:(i + 1) * N_HEADS, :] + carry
            carry = carry + total[i * N_HEADS:(i + 1) * N_HEADS, :]
            bias = jnp.concatenate([jnp.broadcast_to(g[h:h + 1, :], (T, PAGE_SIZE)) for h in range(N_HEADS)], axis=0)
            if with_new and i == 0:
                t = lax.broadcasted_iota(jnp.int32, (R, PAGE_SIZE), 0) % T
                jn = lax.broadcasted_iota(jnp.int32, (R, PAGE_SIZE), 1)
                bias = jnp.where(jn <= t, bias, NEG)
            biases.append(bias)
        carry_ref[...] = carry
        _online_blocks(q, kv, biases, m_ref, l_ref, acc_ref)

    @pl.when(j == 0)
    def _():
        _init_online(m_ref, l_ref, acc_ref)
        carry_ref[...] = jnp.zeros(carry_ref.shape, f32)
        update(True)

    pl.when(j > 0)(functools.partial(update, False))

    @pl.when(j == n_steps - 1)
    def _():
        _finish_heads(o_ref, l_ref, acc_ref, T)


def _sample_b(page_table, q_bd, kv_new, logf_new, pool_kvt, pool_logft, T):
    Bd, n_pages = page_table.shape
    G = _pages_per_step(n_pages, KV_PAGES_PER_STEP)
    n_steps = n_pages // G
    W = W_HEADS
    R = N_HEADS * T
    grid_spec = pltpu.PrefetchScalarGridSpec(
        num_scalar_prefetch=1,
        grid=(Bd, n_steps),
        in_specs=[pl.BlockSpec((1, R, W), lambda b, j, pt: (b, 0, 0)),
                  pl.BlockSpec((1, 2 * W, PAGE_SIZE), lambda b, j, pt: (b, 0, 0)),
                  pl.BlockSpec((1, N_HEADS, PAGE_SIZE), lambda b, j, pt: (b, 0, 0))]
        + _page_specs((1, 2 * W, PAGE_SIZE), n_pages, G, True)
        + _page_specs((1, N_HEADS, PAGE_SIZE), n_pages, G, True),
        out_specs=pl.BlockSpec((1, T, W), lambda b, j, pt: (b, 0, 0)),
        scratch_shapes=[pltpu.VMEM((R, 1), f32), pltpu.VMEM((R, 1), f32), pltpu.VMEM((R, W), f32),
                        pltpu.VMEM((N_HEADS, 1), f32)],
    )
    return pl.pallas_call(
        functools.partial(_sample_b_kernel, n_steps, G),
        grid_spec=grid_spec,
        out_shape=jax.ShapeDtypeStruct((Bd, T, W), bf16),
        compiler_params=pltpu.CompilerParams(dimension_semantics=("arbitrary", "arbitrary"),
                                             vmem_limit_bytes=VMEM_LIMIT),
        name="sample_b",
    )(page_table.reshape(-1), q_bd, kv_new, logf_new, *([pool_kvt] * G), *([pool_logft] * G))


def _rope_tables(pos):
    half = HEAD_DIM // 2
    inv = jnp.power(jnp.float32(ROPE_THETA), -jnp.arange(half, dtype=f32) / half)
    ang = pos.astype(f32)[:, None] * inv[None, :]
    cos = jnp.cos(ang)
    sin = jnp.sin(ang)
    cos64 = jnp.concatenate([cos, cos], axis=1)
    sin64 = jnp.concatenate([-sin, sin], axis=1)
    return jnp.tile(cos64, (1, N_HEADS)), jnp.tile(sin64, (1, N_HEADS)), cos.T, sin.T


def _layer_weights(l, norm1, w_in, b_f, qn_a, kn_a, qn_b, kn_b, w_up_a, w_up_b, w_o, norm2, w_ffn_in, w_ffn_out):
    D = w_in.shape[1]
    W = W_HEADS
    w_t = w_in[l].T
    splits = (W, W, W, W, D_IDX, N_HEADS, W, W, W, N_HEADS, D, D)
    cuts = [int(c) for c in np.cumsum(splits)[:-1]]
    q_a, k_a, v_a, q_i, k_i, w_i, q_b, k_b, v_b, f_b, g_a, g_b = jnp.split(w_t, cuts, axis=0)
    pad_rows = lambda n: jnp.zeros((n, D), w_t.dtype)
    w_row = jnp.concatenate([q_a, q_i, q_b, w_i, pad_rows(LANES - N_HEADS), g_a, g_b], axis=0).T.astype(bf16)
    w_feat = jnp.concatenate([k_a, v_a, k_b, v_b, k_i, f_b, pad_rows(SMALL_ROWS - N_HEADS)], axis=0).astype(bf16)
    d_ff = w_ffn_out.shape[1]
    tile_g = lambda g: jnp.tile(g[l].astype(f32), N_HEADS)[None, :]
    col_g = lambda g: jnp.broadcast_to(g[l].astype(f32)[:, None], (HEAD_DIM, ROW_TILE))
    r = np.arange(W) // HEAD_DIM
    bd = jnp.asarray((r[:, None] == r[None, :]).astype(np.float32)).astype(bf16)
    return {
        "n1": norm1[l][None, :], "w_row": w_row, "w_t": w_feat, "bf_col": b_f[l][:, None],
        "gqa": tile_g(qn_a), "gka": col_g(kn_a), "gqb": tile_g(qn_b), "gkb": col_g(kn_b), "bd": bd,
        "w_up_a": w_up_a[l].astype(bf16), "w_up_b": w_up_b[l].astype(bf16), "w_o": w_o[l].astype(bf16),
        "n2": norm2[l][None, :], "w_gate": w_ffn_in[l][:, :d_ff].astype(bf16),
        "w_upf": w_ffn_in[l][:, d_ff:].astype(bf16), "w_out": w_ffn_out[l].astype(bf16),
    }


def _rows_by_head(x_hm, Bd, T):
    return jnp.transpose(x_hm.reshape(N_HEADS, Bd, T, HEAD_DIM), (1, 0, 2, 3))


def _block_diag_q(x_hm, Bd, T):
    q = _rows_by_head(x_hm, Bd, T)
    eye = jnp.eye(N_HEADS, dtype=q.dtype)
    return jnp.einsum("bhtd,hg->bhtgd", q, eye).reshape(Bd, N_HEADS * T, W_HEADS)


def _new_token_block(x_t, Bd, T):
    r = x_t.shape[1]
    x = jnp.transpose(x_t[0].reshape(r, Bd, T), (1, 0, 2))
    return jnp.pad(x, ((0, 0), (0, 0), (0, PAGE_SIZE - T)))


def _token_major(x_t, Bd, T, feat_shape):
    r = x_t.shape[1]
    return jnp.transpose(x_t[0].reshape(r, Bd, T), (1, 2, 0)).reshape((Bd, T) + feat_shape)


def _slots_last(pool):
    n_pool = pool.shape[0]
    return jnp.moveaxis(pool, 1, -1).reshape(n_pool, -1, PAGE_SIZE)


def kernel(x_prompt, x_sample, cache_a_kv, cache_idx_k, cache_b_kv, cache_b_logf, page_table, norm1, w_in, b_f,
           qn_a, kn_a, qn_b, kn_b, w_up_a, w_up_b, w_o, norm2, w_ffn_in, w_ffn_out):
    B, S, D = x_prompt.shape
    Bd, T, _ = x_sample.shape
    depth = w_in.shape[0]
    n_pages = page_table.shape[1]
    past = n_pages * PAGE_SIZE
    W = W_HEADS
    kv_shape = (2, N_HEADS, HEAD_DIM)
    assert S % ROW_TILE == 0 and Bd * T == ROW_TILE and S % Q_BLOCK == 0
    assert T <= 8

    tab_p = _rope_tables(jnp.arange(S))
    tab_s = _rope_tables(jnp.tile(past + jnp.arange(T), Bd))

    h_p = x_prompt.reshape(B * S, D)
    h_s = x_sample.reshape(Bd * T, D)
    outs = [[] for _ in range(8)]
    for l in range(depth):
        wts = _layer_weights(l, norm1, w_in, b_f, qn_a, kn_a, qn_b, kn_b, w_up_a, w_up_b, w_o, norm2, w_ffn_in,
                             w_ffn_out)
        p = _proj(h_p, S, tab_p, wts)
        o_a = _mixer_a_prompt(p, B, S)
        o_b = _mixer_b_prompt(p, B, S)
        h_p = _merge_ffn(h_p, o_a, o_b, p["ga"], p["gb"], wts)
        outs[0].append(jnp.moveaxis(p["akv"].reshape((B,) + kv_shape + (S,)), -1, 1))
        outs[1].append(jnp.moveaxis(p["idxk"], -1, 1))
        outs[2].append(jnp.moveaxis(p["bkv"].reshape((B,) + kv_shape + (S,)), -1, 1))
        outs[3].append(jnp.moveaxis(p["logf"], -1, 1))

        s = _proj(h_s, Bd * T, tab_s, wts)
        qi_rows = _rows_by_head(s["qi"], Bd, T).reshape(Bd, N_HEADS * T, D_IDX)
        w_col = jnp.transpose(s["aux"][:, 0:N_HEADS].reshape(Bd, T, N_HEADS), (0, 2, 1))
        w_col = w_col.reshape(Bd, N_HEADS * T, 1)
        bias = _sample_index(page_table, qi_rows, w_col, _new_token_block(s["idxk"], Bd, T),
                             _slots_last(cache_idx_k[l]), T)
        o_a = _sample_a(page_table, _block_diag_q(s["qa"], Bd, T), bias, _new_token_block(s["akv"], Bd, T),
                        _slots_last(cache_a_kv[l]), T)
        o_b = _sample_b(page_table, _block_diag_q(s["qb"], Bd, T), _new_token_block(s["bkv"], Bd, T),
                        _new_token_block(s["logf"], Bd, T), _slots_last(cache_b_kv[l]),
                        _slots_last(cache_b_logf[l]), T)
        h_s = _merge_ffn(h_s, o_a.reshape(Bd * T, W), o_b.reshape(Bd * T, W), s["ga"], s["gb"], wts)
        outs[4].append(_token_major(s["akv"], Bd, T, kv_shape))
        outs[5].append(_token_major(s["idxk"], Bd, T, (D_IDX,)))
        outs[6].append(_token_major(s["bkv"], Bd, T, kv_shape))
        outs[7].append(_token_major(s["logf"], Bd, T, (N_HEADS,)))

    return (h_p.reshape(B, S, D), h_s.reshape(Bd, T, D)) + tuple(jnp.stack(o) for o in outs)
```

```python
import functools

import jax
import jax.numpy as jnp
import numpy as np
from jax import lax
from jax.experimental import pallas as pl
from jax.experimental.pallas import tpu as pltpu

HEAD_DIM = 64
N_HEADS = 8
D_IDX = 64
PAGE_SIZE = 128
TOPK_MAX = 256
ROPE_THETA = 10000.0
EPS = 1e-6
ATTN_SCALE = HEAD_DIM ** -0.5
IDX_SCALE = (N_HEADS * D_IDX) ** -0.5
Q_BLOCK = 128
KEY_CHUNK = 512
ROW_TILE = 256
LANES = 128
SMALL_ROWS = 16
NEG = -1e30
VMEM_LIMIT = 56 * 1024 * 1024
INT_MIN = -2 ** 31
KV_PAGES_PER_STEP = 16
IDX_PAGES_PER_STEP = 32

bf16 = jnp.bfloat16
f32 = jnp.float32
W_HEADS = N_HEADS * HEAD_DIM


def _nt_dot(a, b):
    return lax.dot_general(a, b, (((1,), (1,)), ((), ())), preferred_element_type=f32)


def _dot(a, b):
    return jnp.dot(a, b, preferred_element_type=f32)


def _split3(x):
    a1 = x.astype(bf16)
    r1 = x - a1.astype(f32)
    a2 = r1.astype(bf16)
    a3 = (r1 - a2.astype(f32)).astype(bf16)
    return a1, a2, a3


def _log_sigmoid(x):
    return -(jnp.maximum(-x, 0.0) + jnp.log1p(jnp.exp(-jnp.abs(x))))


def _sigmoid(x):
    return 1.0 / (1.0 + jnp.exp(-x))


def _rmsnorm_rows(x, g):
    return x * lax.rsqrt(jnp.mean(x * x, axis=-1, keepdims=True) + EPS) * g


def _head_rmsnorm(x, g, bd):
    x2 = x * x
    hi = x2.astype(bf16)
    lo = (x2 - hi.astype(f32)).astype(bf16)
    ss = _dot(hi, bd) + _dot(lo, bd)
    return x * lax.rsqrt(ss * (1.0 / HEAD_DIM) + EPS) * g


def _rope(x, cos, sin_signed):
    w = x.shape[-1]
    half = HEAD_DIM // 2
    lane = lax.broadcasted_iota(jnp.int32, x.shape, 1)
    first = (lane & half) == 0
    partner = jnp.where(first, pltpu.roll(x, w - half, 1), pltpu.roll(x, half, 1))
    return x * cos + partner * sin_signed


def _head_rmsnorm_t(x, g):
    ms = jnp.mean(x * x, axis=1, keepdims=True)
    return x * lax.rsqrt(ms + EPS) * g[None]


def _rope_t(x, cos, sin):
    half = HEAD_DIM // 2
    x1 = x[:, :half, :]
    x2 = x[:, half:, :]
    return jnp.concatenate([x1 * cos[None] - x2 * sin[None], x2 * cos[None] + x1 * sin[None]], axis=1)


def _proj_kernel(x_ref, n1_ref, wr_ref, wt_ref, bfc_ref, gqa_ref, gka_ref, gqb_ref, gkb_ref,
                 cos_ref, sin_ref, cost_ref, sint_ref, bd_ref,
                 akv_ref, idxk_ref, bkv_ref, logf_ref,
                 qa_ref, qi_ref, qb_ref, aux_ref, ga_ref, gb_ref,
                 kat_ref, vat_ref, kbt_ref, vbt_ref, kit_ref):
    W = W_HEADS
    xn = _rmsnorm_rows(x_ref[...], n1_ref[...])
    xb = xn.astype(bf16)
    bd = bd_ref[...]
    cos = cos_ref[...]
    sin = sin_ref[...]
    cos_t = cost_ref[...]
    sin_t = sint_ref[...]
    n = xb.shape[0]

    def heads_out(ref, val):
        vb = val.astype(bf16)
        for h in range(N_HEADS):
            ref[h] = vb[:, h * HEAD_DIM:(h + 1) * HEAD_DIM]

    q_a = _dot(xb, wr_ref[:, 0:W])
    heads_out(qa_ref, _rope(_head_rmsnorm(q_a, gqa_ref[...], bd), cos, sin) * ATTN_SCALE)
    heads_out(qi_ref, _rope(_dot(xb, wr_ref[:, W:2 * W]), cos, sin))
    q_b = _dot(xb, wr_ref[:, 2 * W:3 * W])
    heads_out(qb_ref, _head_rmsnorm(q_b, gqb_ref[...], bd) * ATTN_SCALE)
    c0 = 3 * W
    aux_ref[...] = _dot(xb, wr_ref[:, c0:c0 + LANES])
    c0 += LANES
    D = ga_ref.shape[1]
    ga_ref[...] = _dot(xb, wr_ref[:, c0:c0 + D])
    gb_ref[...] = _dot(xb, wr_ref[:, c0 + D:c0 + 2 * D])

    def t_proj(r0, rows):
        return _nt_dot(wt_ref[r0:r0 + rows, :], xb)

    k_a = _rope_t(_head_rmsnorm_t(t_proj(0, W).reshape(N_HEADS, HEAD_DIM, n), gka_ref[...]), cos_t, sin_t)
    k_a = k_a.reshape(W, n)
    akv_ref[0, 0:W, :] = k_a
    kat_ref[...] = k_a.astype(bf16)
    v_a = t_proj(W, W)
    akv_ref[0, W:2 * W, :] = v_a
    vat_ref[...] = v_a.astype(bf16)
    k_b = _head_rmsnorm_t(t_proj(2 * W, W).reshape(N_HEADS, HEAD_DIM, n), gkb_ref[...]).reshape(W, n)
    bkv_ref[0, 0:W, :] = k_b
    kbt_ref[...] = k_b.astype(bf16)
    v_b = t_proj(3 * W, W)
    bkv_ref[0, W:2 * W, :] = v_b
    vbt_ref[...] = v_b.astype(bf16)
    k_i = _rope_t(t_proj(4 * W, D_IDX)[None], cos_t, sin_t)[0]
    idxk_ref[0] = k_i
    kit_ref[...] = k_i.astype(bf16)
    f = t_proj(4 * W + D_IDX, SMALL_ROWS)[0:N_HEADS, :]
    logf_ref[0] = _log_sigmoid(f + bfc_ref[...])


def _const_spec(shape):
    nd = len(shape)
    return pl.BlockSpec(shape, lambda *_: (0,) * nd, pipeline_mode=pl.Buffered(1))


def _proj(x, rows_per_batch, tables, wts):
    N, D = x.shape
    W = W_HEADS
    tm = ROW_TILE
    rpb = rows_per_batch
    nb = N // rpb
    nt = rpb // tm
    cos, sin, cos_t, sin_t = tables
    row = lambda w: pl.BlockSpec((tm, w), lambda i: (i, 0))
    tab = lambda w: pl.BlockSpec((tm, w), lambda i: (i % nt, 0))
    tab_t = pl.BlockSpec((HEAD_DIM // 2, tm), lambda i: (0, i % nt))
    hm = pl.BlockSpec((N_HEADS, tm, HEAD_DIM), lambda i: (0, i, 0))
    fm = lambda r: pl.BlockSpec((1, r, tm), lambda i: (i // nt, 0, i % nt))
    fm2 = lambda r: pl.BlockSpec((r, tm), lambda i: (0, i))
    consts = [wts["n1"], wts["w_row"], wts["w_t"], wts["bf_col"], wts["gqa"], wts["gka"], wts["gqb"], wts["gkb"]]
    in_specs = ([row(D)] + [_const_spec(c.shape) for c in consts]
                + [tab(W), tab(W), tab_t, tab_t, _const_spec(wts["bd"].shape)])
    sds = jax.ShapeDtypeStruct
    out_shape = [sds((nb, 2 * W, rpb), f32), sds((nb, D_IDX, rpb), f32), sds((nb, 2 * W, rpb), f32),
                 sds((nb, N_HEADS, rpb), f32)]
    out_specs = [fm(2 * W), fm(D_IDX), fm(2 * W), fm(N_HEADS)]
    out_shape += [sds((N_HEADS, N, HEAD_DIM), bf16)] * 3 + [sds((N, LANES), f32), sds((N, D), f32), sds((N, D), f32)]
    out_specs += [hm] * 3 + [row(LANES), row(D), row(D)]
    out_shape += [sds((W, N), bf16)] * 4 + [sds((D_IDX, N), bf16)]
    out_specs += [fm2(W)] * 4 + [fm2(D_IDX)]
    names = ["akv", "idxk", "bkv", "logf", "qa", "qi", "qb", "aux", "ga", "gb", "kat", "vat", "kbt", "vbt", "kit"]
    outs = pl.pallas_call(
        _proj_kernel,
        grid=(N // tm,),
        in_specs=in_specs,
        out_specs=out_specs,
        out_shape=out_shape,
        compiler_params=pltpu.CompilerParams(dimension_semantics=("arbitrary",), vmem_limit_bytes=VMEM_LIMIT),
        name="proj",
    )(x, *consts, cos, sin, cos_t, sin_t, wts["bd"])
    return dict(zip(names, outs))


def _ordinal_to_float(o):
    bits = jnp.where(o < 0, o ^ jnp.int32(0x7FFFFFFF), o)
    return lax.bitcast_convert_type(bits, f32)


def _count(mask):
    ones = jnp.where(mask, 1.0, 0.0)
    n = ones.shape[1]
    parts = [ones[:, c:min(c + LANES, n)] for c in range(0, n, LANES)]
    while len(parts) > 1:
        parts = [parts[a] + parts[a + 1] for a in range(0, len(parts) - 1, 2)] + (parts[-1:] if len(parts) % 2 else [])
    return jnp.sum(parts[0], axis=1, keepdims=True)


def _topk_mask(score_ref, k, idx_bits):
    R, L = score_ref.shape
    kf = jnp.float32(k)

    zero = jnp.zeros((R, 1), jnp.int32)
    t0 = jnp.where(_count(score_ref[...] >= 0.0) >= kf, zero, jnp.full((R, 1), INT_MIN, jnp.int32))

    def vbody(i, t):
        cand = t | jnp.left_shift(jnp.int32(1), 30 - i)
        return jnp.where(_count(score_ref[...] >= _ordinal_to_float(cand)) >= kf, cand, t)

    t_ord = lax.fori_loop(0, 31, vbody, t0, unroll=R > 8)
    tiny = jnp.float32(np.finfo(np.float32).tiny)
    flush = lambda x: jnp.where(jnp.abs(x) < tiny, 0.0, x)
    thr = flush(_ordinal_to_float(t_ord))
    nxt = jnp.where(thr == 0.0, tiny, flush(_ordinal_to_float(t_ord + 1)))

    score = score_ref[...]
    take_all = _count(score > -jnp.inf) <= kf
    n_above = _count(score >= nxt)
    n_ties = _count(score >= thr) - n_above
    lane = lax.broadcasted_iota(jnp.int32, (R, L), 1)

    def search():
        def vref(i, c):
            lo, hi = c
            mid = 0.5 * lo + 0.5 * hi
            ok = _count(score_ref[...] >= mid) >= kf
            return jnp.where(ok, mid, lo), jnp.where(ok, hi, mid)

        mid0 = 0.5 * thr + 0.5 * nxt
        interior = jnp.max(jnp.where((mid0 > thr) & (mid0 < nxt), 1.0, 0.0)) > 0.0
        lo, hi = lax.cond(interior, lambda: lax.fori_loop(0, 24, vref, (thr, nxt)), lambda: (thr, nxt))
        need = kf - _count(score_ref[...] >= hi)

        def ibody(i, c):
            a, b = c
            mid = (a + b) >> 1
            s = score_ref[...]
            ok = _count((s >= lo) & (s < hi) & (lane <= mid)) >= need
            return jnp.where(ok, a, mid), jnp.where(ok, mid, b)

        a0 = jnp.full((R, 1), -1, jnp.int32)
        b0 = jnp.full((R, 1), L - 1, jnp.int32)
        return lo, hi, lax.fori_loop(0, idx_bits, ibody, (a0, b0))[1]

    straddle = jnp.where(take_all, -1.0, n_ties - (kf - n_above))
    lo, hi, cut = lax.cond(jnp.max(straddle) > 0.0, search, lambda: (thr, nxt, jnp.full((R, 1), L - 1, jnp.int32)))
    score = score_ref[...]
    return (score >= hi) | ((score >= lo) & (score < hi) & (lane <= cut)) | take_all


def _key_classes(S, tq):
    ch = min(KEY_CHUNK, S)
    return list(range(ch, S + 1, ch))


def _for_key_class(i, tq, classes, body):
    need = (i + 1) * tq
    lo = 0
    for n in classes:
        pl.when((need > lo) & (need <= n))(functools.partial(body, n))
        lo = n


def _softmax_pv_t(s, vt):
    m = jnp.max(s, axis=1, keepdims=True)
    p = jnp.exp(s - m)
    l = jnp.sum(p, axis=1, keepdims=True)
    return _nt_dot(p.astype(bf16), vt) / l


def _mixer_a_prompt_kernel(topk, idx_bits, classes, qi_ref, aux_ref, kit_ref, qa_ref, kat_ref, vat_ref, o_ref,
                           score_ref, bias_ref):
    i = pl.program_id(1)
    tq = qi_ref.shape[1]

    def body(n):
        kit = kit_ref[:, 0:n]
        aux = aux_ref[...]
        score = jnp.zeros((tq, n), f32)
        for h in range(N_HEADS):
            dots = _dot(qi_ref[h], kit)
            w_h = aux[:, h:h + 1] * IDX_SCALE
            score = score + jnp.maximum(dots, 0.0) * w_h
        qpos = i * tq + lax.broadcasted_iota(jnp.int32, (tq, n), 0)
        kpos = lax.broadcasted_iota(jnp.int32, (tq, n), 1)
        causal = kpos <= qpos
        masked = score_ref.at[:, 0:n]
        masked[...] = jnp.where(causal, score, -jnp.inf)
        sel = _topk_mask(masked, topk, idx_bits)
        bias_ref[:, 0:n] = jnp.where(sel & causal, 0.0, -jnp.inf)
        outs = []
        for h in range(N_HEADS):
            r = slice(h * HEAD_DIM, (h + 1) * HEAD_DIM)
            s = _dot(qa_ref[h], kat_ref[r, 0:n]) + bias_ref[:, 0:n]
            outs.append(_softmax_pv_t(s, vat_ref[r, 0:n]))
        o_ref[...] = jnp.concatenate(outs, axis=1).astype(bf16)

    _for_key_class(i, tq, classes, body)


def _mixer_a_prompt(p, B, S):
    N = B * S
    W = W_HEADS
    tq = Q_BLOCK
    nq = S // tq
    topk = min(TOPK_MAX, S // 4)
    idx_bits = int(np.ceil(np.log2(S)))
    qblk = pl.BlockSpec((N_HEADS, tq, HEAD_DIM), lambda b, i: (0, b * nq + i, 0))
    seq = lambda r: pl.BlockSpec((r, S), lambda b, i: (0, b))
    return pl.pallas_call(
        functools.partial(_mixer_a_prompt_kernel, topk, idx_bits, _key_classes(S, tq)),
        grid=(B, nq),
        in_specs=[qblk, pl.BlockSpec((tq, LANES), lambda b, i: (b * nq + i, 0)), seq(D_IDX), qblk, seq(W), seq(W)],
        out_specs=pl.BlockSpec((tq, W), lambda b, i: (b * nq + i, 0)),
        out_shape=jax.ShapeDtypeStruct((N, W), bf16),
        scratch_shapes=[pltpu.VMEM((tq, S), f32), pltpu.VMEM((tq, S), f32)],
        compiler_params=pltpu.CompilerParams(dimension_semantics=("arbitrary", "arbitrary"),
                                             vmem_limit_bytes=VMEM_LIMIT),
        name="mixer_a_prompt",
    )(p["qi"], p["aux"], p["kit"], p["qa"], p["kat"], p["vat"])


def _suffix_bias(logf_row, carry, strict_upper):
    a1, a2, a3 = _split3(logf_row)
    inner = _dot(a1, strict_upper) + _dot(a2, strict_upper) + _dot(a3, strict_upper)
    return inner + carry, carry + jnp.sum(logf_row, axis=1, keepdims=True)


def _strict_lower_ones(n):
    r = lax.broadcasted_iota(jnp.int32, (n, n), 0)
    c = lax.broadcasted_iota(jnp.int32, (n, n), 1)
    return jnp.where(r > c, 1.0, 0.0).astype(bf16)


def _mixer_b_prompt_kernel(classes, logf_ref, qb_ref, kbt_ref, vbt_ref, o_ref, g_ref):
    i = pl.program_id(1)
    tq = qb_ref.shape[1]
    S = kbt_ref.shape[1]

    @pl.when(i == 0)
    def _():
        u = _strict_lower_ones(LANES)
        carry = jnp.zeros((N_HEADS, 1), f32)
        for c in reversed(range(S // LANES)):
            g, carry = _suffix_bias(logf_ref[0, :, c * LANES:(c + 1) * LANES], carry, u)
            g_ref[:, c * LANES:(c + 1) * LANES] = g

    def body(n):
        qpos = i * tq + lax.broadcasted_iota(jnp.int32, (tq, n), 0)
        kpos = lax.broadcasted_iota(jnp.int32, (tq, n), 1)
        causal = kpos <= qpos
        outs = []
        for h in range(N_HEADS):
            r = slice(h * HEAD_DIM, (h + 1) * HEAD_DIM)
            s = _dot(qb_ref[h], kbt_ref[r, 0:n]) + g_ref[h:h + 1, 0:n]
            s = jnp.where(causal, s, -jnp.inf)
            outs.append(_softmax_pv_t(s, vbt_ref[r, 0:n]))
        o_ref[...] = jnp.concatenate(outs, axis=1).astype(bf16)

    _for_key_class(i, tq, classes, body)


def _mixer_b_prompt(p, B, S):
    N = B * S
    W = W_HEADS
    tq = Q_BLOCK
    nq = S // tq
    qblk = pl.BlockSpec((N_HEADS, tq, HEAD_DIM), lambda b, i: (0, b * nq + i, 0))
    seq = pl.BlockSpec((W, S), lambda b, i: (0, b))
    return pl.pallas_call(
        functools.partial(_mixer_b_prompt_kernel, _key_classes(S, tq)),
        grid=(B, nq),
        in_specs=[pl.BlockSpec((1, N_HEADS, S), lambda b, i: (b, 0, 0)), qblk, seq, seq],
        out_specs=pl.BlockSpec((tq, W), lambda b, i: (b * nq + i, 0)),
        out_shape=jax.ShapeDtypeStruct((N, W), bf16),
        scratch_shapes=[pltpu.VMEM((N_HEADS, S), f32)],
        compiler_params=pltpu.CompilerParams(dimension_semantics=("arbitrary", "arbitrary"),
                                             vmem_limit_bytes=VMEM_LIMIT),
        name="mixer_b_prompt",
    )(p["logf"], p["qb"], p["kbt"], p["vbt"])


def _merge_ffn_kernel(n_ff_chunks, x_ref, oa_ref, ob_ref, ga_ref, gb_ref, wua_ref, wub_ref, wo_ref, n2_ref,
                      wg_ref, wu_ref, wout_ref, y_ref):
    u_a = _dot(oa_ref[...], wua_ref[...])
    u_b = _dot(ob_ref[...], wub_ref[...])
    m = _sigmoid(ga_ref[...]) * u_a + _sigmoid(gb_ref[...]) * u_b
    x1 = x_ref[...] + _dot(m.astype(bf16), wo_ref[...])
    hn = _rmsnorm_rows(x1, n2_ref[...]).astype(bf16)
    d_ff = wg_ref.shape[1]
    cw = d_ff // n_ff_chunks
    acc = x1
    for c in range(n_ff_chunks):
        gate = _dot(hn, wg_ref[:, c * cw:(c + 1) * cw])
        up = _dot(hn, wu_ref[:, c * cw:(c + 1) * cw])
        act = gate * _sigmoid(gate) * up
        acc = acc + _dot(act.astype(bf16), wout_ref[c * cw:(c + 1) * cw, :])
    y_ref[...] = acc


def _merge_ffn(x, o_a, o_b, g_a, g_b, wts):
    N, D = x.shape
    W = W_HEADS
    tm = ROW_TILE
    row = lambda w: pl.BlockSpec((tm, w), lambda i: (i, 0))
    consts = [wts["w_up_a"], wts["w_up_b"], wts["w_o"], wts["n2"], wts["w_gate"], wts["w_upf"], wts["w_out"]]
    d_ff = wts["w_gate"].shape[1]
    n_chunks = 2 if d_ff % (2 * LANES) == 0 else 1
    return pl.pallas_call(
        functools.partial(_merge_ffn_kernel, n_chunks),
        grid=(N // tm,),
        in_specs=[row(D), row(W), row(W), row(D), row(D)] + [_const_spec(c.shape) for c in consts],
        out_specs=row(D),
        out_shape=jax.ShapeDtypeStruct((N, D), f32),
        compiler_params=pltpu.CompilerParams(dimension_semantics=("arbitrary",), vmem_limit_bytes=VMEM_LIMIT),
        name="merge_ffn",
    )(x, o_a, o_b, g_a, g_b, *consts)


def _pages_per_step(n_pages, want):
    g = min(want, n_pages)
    assert n_pages % g == 0
    return g


def _page_specs(block, n_pages, G, reverse):
    n_steps = n_pages // G
    specs = []
    for g in range(G):
        if reverse:
            fn = lambda b, j, pt, g=g: (pt[b * n_pages + (n_steps - 1 - j) * G + g], 0, 0)
        else:
            fn = lambda b, j, pt, g=g: (pt[b * n_pages + j * G + g], 0, 0)
        specs.append(pl.BlockSpec(block, fn))
    return specs


def _sample_index_kernel(topk, idx_bits, n_steps, G, pt_ref, qi_ref, w_ref, kin_ref, *rest):
    pages = rest[:G]
    bias_ref, score_ref = rest[G:]
    j = pl.program_id(1)
    T = score_ref.shape[0]
    qi = qi_ref[0]
    w = w_ref[0] * IDX_SCALE

    def page_scores(kits):
        dots = jnp.concatenate([_dot(qi, k.astype(bf16)) for k in kits], axis=1)
        contrib = jnp.maximum(dots, 0.0) * w
        return jnp.sum(contrib.reshape(N_HEADS, T, dots.shape[1]), axis=0)

    off = pl.multiple_of(j * (G * PAGE_SIZE), G * PAGE_SIZE)
    score_ref[:, pl.ds(off, G * PAGE_SIZE)] = page_scores([pages[g][0] for g in range(G)])

    @pl.when(j == n_steps - 1)
    def _():
        past = n_steps * G * PAGE_SIZE
        L = score_ref.shape[1]
        sc = page_scores([kin_ref[0]])
        t = lax.broadcasted_iota(jnp.int32, (T, PAGE_SIZE), 0)
        jn = lax.broadcasted_iota(jnp.int32, (T, PAGE_SIZE), 1)
        score_ref[:, past:past + PAGE_SIZE] = jnp.where(jn <= t, sc, -jnp.inf)
        sel = _topk_mask(score_ref, topk, idx_bits)
        kpos = lax.broadcasted_iota(jnp.int32, (T, L), 1)
        qpos = past + lax.broadcasted_iota(jnp.int32, (T, L), 0)
        bias_ref[0] = jnp.where(sel & (kpos <= qpos), 0.0, NEG)


def _sample_index(page_table, qi_rows, w_col, ki_new, pool_kit, T):
    Bd, n_pages = page_table.shape
    G = _pages_per_step(n_pages, IDX_PAGES_PER_STEP)
    n_steps = n_pages // G
    past = n_pages * PAGE_SIZE
    L = past + PAGE_SIZE
    topk = min(TOPK_MAX, (past + T) // 4)
    idx_bits = int(np.ceil(np.log2(L)))
    R = N_HEADS * T
    grid_spec = pltpu.PrefetchScalarGridSpec(
        num_scalar_prefetch=1,
        grid=(Bd, n_steps),
        in_specs=[pl.BlockSpec((1, R, D_IDX), lambda b, j, pt: (b, 0, 0)),
                  pl.BlockSpec((1, R, 1), lambda b, j, pt: (b, 0, 0)),
                  pl.BlockSpec((1, D_IDX, PAGE_SIZE), lambda b, j, pt: (b, 0, 0))]
        + _page_specs((1, D_IDX, PAGE_SIZE), n_pages, G, False),
        out_specs=pl.BlockSpec((1, T, L), lambda b, j, pt: (b, 0, 0)),
        scratch_shapes=[pltpu.VMEM((T, L), f32)],
    )
    return pl.pallas_call(
        functools.partial(_sample_index_kernel, topk, idx_bits, n_steps, G),
        grid_spec=grid_spec,
        out_shape=jax.ShapeDtypeStruct((Bd, T, L), f32),
        compiler_params=pltpu.CompilerParams(dimension_semantics=("arbitrary", "arbitrary"),
                                             vmem_limit_bytes=VMEM_LIMIT),
        name="sample_index",
    )(page_table.reshape(-1), qi_rows, w_col, ki_new, *([pool_kit] * G))


def _online_blocks(q, kv_refs, biases, m_ref, l_ref, acc_ref):
    W = W_HEADS
    s = jnp.concatenate([_dot(q, r[0, 0:W, :].astype(bf16)) + b for r, b in zip(kv_refs, biases)], axis=1)
    m_old = m_ref[...]
    m_new = jnp.maximum(m_old, jnp.max(s, axis=1, keepdims=True))
    alpha = jnp.exp(m_old - m_new)
    p = jnp.exp(s - m_new)
    l_ref[...] = alpha * l_ref[...] + jnp.sum(p, axis=1, keepdims=True)
    pb = p.astype(bf16)
    pv = None
    for g, r in enumerate(kv_refs):
        d = _nt_dot(pb[:, g * PAGE_SIZE:(g + 1) * PAGE_SIZE], r[0, W:2 * W, :].astype(bf16))
        pv = d if pv is None else pv + d
    acc_ref[...] = alpha * acc_ref[...] + pv
    m_ref[...] = m_new


def _finish_heads(o_ref, l_ref, acc_ref, T):
    R, W = acc_ref.shape
    rh = lax.broadcasted_iota(jnp.int32, (R, W), 0) // T
    ch = lax.broadcasted_iota(jnp.int32, (R, W), 1) // HEAD_DIM
    o = jnp.where(rh == ch, acc_ref[...] / l_ref[...], 0.0)
    o_ref[0] = jnp.sum(o.reshape(N_HEADS, T, W), axis=0).astype(bf16)


def _init_online(m_ref, l_ref, acc_ref):
    m_ref[...] = jnp.full(m_ref.shape, NEG, f32)
    l_ref[...] = jnp.zeros(l_ref.shape, f32)
    acc_ref[...] = jnp.zeros(acc_ref.shape, f32)


def _sample_a_kernel(n_steps, G, pt_ref, q_ref, bias_ref, kvn_ref, *rest):
    pages = rest[:G]
    o_ref, m_ref, l_ref, acc_ref = rest[G:]
    j = pl.program_id(1)
    T = bias_ref.shape[1]
    q = q_ref[0]

    @pl.when(j == 0)
    def _():
        _init_online(m_ref, l_ref, acc_ref)

    def update(with_new):
        off = pl.multiple_of(j * (G * PAGE_SIZE), G * PAGE_SIZE)
        b_all = bias_ref[0, :, pl.ds(off, G * PAGE_SIZE)]
        refs = list(pages)
        biases = [jnp.tile(b_all[:, g * PAGE_SIZE:(g + 1) * PAGE_SIZE], (N_HEADS, 1)) for g in range(G)]
        if with_new:
            past = n_steps * G * PAGE_SIZE
            refs.append(kvn_ref)
            biases.append(jnp.tile(bias_ref[0, :, past:past + PAGE_SIZE], (N_HEADS, 1)))
        _online_blocks(q, refs, biases, m_ref, l_ref, acc_ref)

    pl.when(j < n_steps - 1)(functools.partial(update, False))

    @pl.when(j == n_steps - 1)
    def _():
        update(True)
        _finish_heads(o_ref, l_ref, acc_ref, T)


def _sample_a(page_table, q_bd, bias, kv_new, pool_kvt, T):
    Bd, n_pages = page_table.shape
    G = _pages_per_step(n_pages, KV_PAGES_PER_STEP)
    n_steps = n_pages // G
    W = W_HEADS
    R = N_HEADS * T
    L = bias.shape[2]
    grid_spec = pltpu.PrefetchScalarGridSpec(
        num_scalar_prefetch=1,
        grid=(Bd, n_steps),
        in_specs=[pl.BlockSpec((1, R, W), lambda b, j, pt: (b, 0, 0)),
                  pl.BlockSpec((1, T, L), lambda b, j, pt: (b, 0, 0)),
                  pl.BlockSpec((1, 2 * W, PAGE_SIZE), lambda b, j, pt: (b, 0, 0))]
        + _page_specs((1, 2 * W, PAGE_SIZE), n_pages, G, False),
        out_specs=pl.BlockSpec((1, T, W), lambda b, j, pt: (b, 0, 0)),
        scratch_shapes=[pltpu.VMEM((R, 1), f32), pltpu.VMEM((R, 1), f32), pltpu.VMEM((R, W), f32)],
    )
    return pl.pallas_call(
        functools.partial(_sample_a_kernel, n_steps, G),
        grid_spec=grid_spec,
        out_shape=jax.ShapeDtypeStruct((Bd, T, W), bf16),
        compiler_params=pltpu.CompilerParams(dimension_semantics=("arbitrary", "arbitrary"),
                                             vmem_limit_bytes=VMEM_LIMIT),
        name="sample_a",
    )(page_table.reshape(-1), q_bd, bias, kv_new, *([pool_kvt] * G))


def _sample_b_kernel(n_steps, G, pt_ref, q_ref, kvn_ref, lfn_ref, *rest):
    kv_pages = rest[:G]
    lf_pages = rest[G:2 * G]
    o_ref, m_ref, l_ref, acc_ref, carry_ref = rest[2 * G:]
    j = pl.program_id(1)
    R = q_ref.shape[1]
    T = R // N_HEADS
    q = q_ref[0]

    def update(with_new):
        kv = [kv_pages[g] for g in reversed(range(G))]
        lf = [lf_pages[g][0] for g in reversed(range(G))]
        if with_new:
            kv = [kvn_ref] + kv
            lf = [lfn_ref[0]] + lf
        lf_all = jnp.concatenate(lf, axis=0)
        inner, _ = _suffix_bias(lf_all, 0.0, _strict_lower_ones(PAGE_SIZE))
        total = jnp.sum(lf_all, axis=1, keepdims=True)
        carry = carry_ref[...]
        biases = []
        for i in range(len(kv)):
            g = inner[i * N_HEADS:(i + 1) * N_HEADS, :] + carry
            carry = carry + total[i * N_HEADS:(i + 1) * N_HEADS, :]
            bias = jnp.concatenate([jnp.broadcast_to(g[h:h + 1, :], (T, PAGE_SIZE)) for h in range(N_HEADS)], axis=0)
            if with_new and i == 0:
                t = lax.broadcasted_iota(jnp.int32, (R, PAGE_SIZE), 0) % T
                jn = lax.broadcasted_iota(jnp.int32, (R, PAGE_SIZE), 1)
                bias = jnp.where(jn <= t, bias, NEG)
            biases.append(bias)
        carry_ref[...] = carry
        _online_blocks(q, kv, biases, m_ref, l_ref, acc_ref)

    @pl.when(j == 0)
    def _():
        _init_online(m_ref, l_ref, acc_ref)
        carry_ref[...] = jnp.zeros(carry_ref.shape, f32)
        update(True)

    pl.when(j > 0)(functools.partial(update, False))

    @pl.when(j == n_steps - 1)
    def _():
        _finish_heads(o_ref, l_ref, acc_ref, T)


def _sample_b(page_table, q_bd, kv_new, logf_new, pool_kvt, pool_logft, T):
    Bd, n_pages = page_table.shape
    G = _pages_per_step(n_pages, KV_PAGES_PER_STEP)
    n_steps = n_pages // G
    W = W_HEADS
    R = N_HEADS * T
    grid_spec = pltpu.PrefetchScalarGridSpec(
        num_scalar_prefetch=1,
        grid=(Bd, n_steps),
        in_specs=[pl.BlockSpec((1, R, W), lambda b, j, pt: (b, 0, 0)),
                  pl.BlockSpec((1, 2 * W, PAGE_SIZE), lambda b, j, pt: (b, 0, 0)),
                  pl.BlockSpec((1, N_HEADS, PAGE_SIZE), lambda b, j, pt: (b, 0, 0))]
        + _page_specs((1, 2 * W, PAGE_SIZE), n_pages, G, True)
        + _page_specs((1, N_HEADS, PAGE_SIZE), n_pages, G, True),
        out_specs=pl.BlockSpec((1, T, W), lambda b, j, pt: (b, 0, 0)),
        scratch_shapes=[pltpu.VMEM((R, 1), f32), pltpu.VMEM((R, 1), f32), pltpu.VMEM((R, W), f32),
                        pltpu.VMEM((N_HEADS, 1), f32)],
    )
    return pl.pallas_call(
        functools.partial(_sample_b_kernel, n_steps, G),
        grid_spec=grid_spec,
        out_shape=jax.ShapeDtypeStruct((Bd, T, W), bf16),
        compiler_params=pltpu.CompilerParams(dimension_semantics=("arbitrary", "arbitrary"),
                                             vmem_limit_bytes=VMEM_LIMIT),
        name="sample_b",
    )(page_table.reshape(-1), q_bd, kv_new, logf_new, *([pool_kvt] * G), *([pool_logft] * G))


def _rope_tables(pos):
    half = HEAD_DIM // 2
    inv = jnp.power(jnp.float32(ROPE_THETA), -jnp.arange(half, dtype=f32) / half)
    ang = pos.astype(f32)[:, None] * inv[None, :]
    cos = jnp.cos(ang)
    sin = jnp.sin(ang)
    cos64 = jnp.concatenate([cos, cos], axis=1)
    sin64 = jnp.concatenate([-sin, sin], axis=1)
    return jnp.tile(cos64, (1, N_HEADS)), jnp.tile(sin64, (1, N_HEADS)), cos.T, sin.T


def _layer_weights(l, norm1, w_in, b_f, qn_a, kn_a, qn_b, kn_b, w_up_a, w_up_b, w_o, norm2, w_ffn_in, w_ffn_out):
    D = w_in.shape[1]
    W = W_HEADS
    w_t = w_in[l].T
    splits = (W, W, W, W, D_IDX, N_HEADS, W, W, W, N_HEADS, D, D)
    cuts = [int(c) for c in np.cumsum(splits)[:-1]]
    q_a, k_a, v_a, q_i, k_i, w_i, q_b, k_b, v_b, f_b, g_a, g_b = jnp.split(w_t, cuts, axis=0)
    pad_rows = lambda n: jnp.zeros((n, D), w_t.dtype)
    w_row = jnp.concatenate([q_a, q_i, q_b, w_i, pad_rows(LANES - N_HEADS), g_a, g_b], axis=0).T.astype(bf16)
    w_feat = jnp.concatenate([k_a, v_a, k_b, v_b, k_i, f_b, pad_rows(SMALL_ROWS - N_HEADS)], axis=0).astype(bf16)
    d_ff = w_ffn_out.shape[1]
    tile_g = lambda g: jnp.tile(g[l].astype(f32), N_HEADS)[None, :]
    col_g = lambda g: jnp.broadcast_to(g[l].astype(f32)[:, None], (HEAD_DIM, ROW_TILE))
    r = np.arange(W) // HEAD_DIM
    bd = jnp.asarray((r[:, None] == r[None, :]).astype(np.float32)).astype(bf16)
    return {
        "n1": norm1[l][None, :], "w_row": w_row, "w_t": w_feat, "bf_col": b_f[l][:, None],
        "gqa": tile_g(qn_a), "gka": col_g(kn_a), "gqb": tile_g(qn_b), "gkb": col_g(kn_b), "bd": bd,
        "w_up_a": w_up_a[l].astype(bf16), "w_up_b": w_up_b[l].astype(bf16), "w_o": w_o[l].astype(bf16),
        "n2": norm2[l][None, :], "w_gate": w_ffn_in[l][:, :d_ff].astype(bf16),
        "w_upf": w_ffn_in[l][:, d_ff:].astype(bf16), "w_out": w_ffn_out[l].astype(bf16),
    }


def _rows_by_head(x_hm, Bd, T):
    return jnp.transpose(x_hm.reshape(N_HEADS, Bd, T, HEAD_DIM), (1, 0, 2, 3))


def _block_diag_q(x_hm, Bd, T):
    q = _rows_by_head(x_hm, Bd, T)
    eye = jnp.eye(N_HEADS, dtype=q.dtype)
    return jnp.einsum("bhtd,hg->bhtgd", q, eye).reshape(Bd, N_HEADS * T, W_HEADS)


def _new_token_block(x_t, Bd, T):
    r = x_t.shape[1]
    x = jnp.transpose(x_t[0].reshape(r, Bd, T), (1, 0, 2))
    return jnp.pad(x, ((0, 0), (0, 0), (0, PAGE_SIZE - T)))


def _token_major(x_t, Bd, T, feat_shape):
    r = x_t.shape[1]
    return jnp.transpose(x_t[0].reshape(r, Bd, T), (1, 2, 0)).reshape((Bd, T) + feat_shape)


def _slots_last(pool):
    n_pool = pool.shape[0]
    return jnp.moveaxis(pool, 1, -1).reshape(n_pool, -1, PAGE_SIZE)


def kernel(x_prompt, x_sample, cache_a_kv, cache_idx_k, cache_b_kv, cache_b_logf, page_table, norm1, w_in, b_f,
           qn_a, kn_a, qn_b, kn_b, w_up_a, w_up_b, w_o, norm2, w_ffn_in, w_ffn_out):
    B, S, D = x_prompt.shape
    Bd, T, _ = x_sample.shape
    depth = w_in.shape[0]
    n_pages = page_table.shape[1]
    past = n_pages * PAGE_SIZE
    W = W_HEADS
    kv_shape = (2, N_HEADS, HEAD_DIM)
    assert S % ROW_TILE == 0 and Bd * T == ROW_TILE and S % Q_BLOCK == 0
    assert T <= 8

    tab_p = _rope_tables(jnp.arange(S))
    tab_s = _rope_tables(jnp.tile(past + jnp.arange(T), Bd))

    h_p = x_prompt.reshape(B * S, D)
    h_s = x_sample.reshape(Bd * T, D)
    outs = [[] for _ in range(8)]
    for l in range(depth):
        wts = _layer_weights(l, norm1, w_in, b_f, qn_a, kn_a, qn_b, kn_b, w_up_a, w_up_b, w_o, norm2, w_ffn_in,
                             w_ffn_out)
        p = _proj(h_p, S, tab_p, wts)
        o_a = _mixer_a_prompt(p, B, S)
        o_b = _mixer_b_prompt(p, B, S)
        h_p = _merge_ffn(h_p, o_a, o_b, p["ga"], p["gb"], wts)
        outs[0].append(jnp.moveaxis(p["akv"].reshape((B,) + kv_shape + (S,)), -1, 1))
        outs[1].append(jnp.moveaxis(p["idxk"], -1, 1))
        outs[2].append(jnp.moveaxis(p["bkv"].reshape((B,) + kv_shape + (S,)), -1, 1))
        outs[3].append(jnp.moveaxis(p["logf"], -1, 1))

        s = _proj(h_s, Bd * T, tab_s, wts)
        qi_rows = _rows_by_head(s["qi"], Bd, T).reshape(Bd, N_HEADS * T, D_IDX)
        w_col = jnp.transpose(s["aux"][:, 0:N_HEADS].reshape(Bd, T, N_HEADS), (0, 2, 1))
        w_col = w_col.reshape(Bd, N_HEADS * T, 1)
        bias = _sample_index(page_table, qi_rows, w_col, _new_token_block(s["idxk"], Bd, T),
                             _slots_last(cache_idx_k[l]), T)
        o_a = _sample_a(page_table, _block_diag_q(s["qa"], Bd, T), bias, _new_token_block(s["akv"], Bd, T),
                        _slots_last(cache_a_kv[l]), T)
        o_b = _sample_b(page_table, _block_diag_q(s["qb"], Bd, T), _new_token_block(s["bkv"], Bd, T),
                        _new_token_block(s["logf"], Bd, T), _slots_last(cache_b_kv[l]),
                        _slots_last(cache_b_logf[l]), T)
        h_s = _merge_ffn(h_s, o_a.reshape(Bd * T, W), o_b.reshape(Bd * T, W), s["ga"], s["gb"], wts)
        outs[4].append(_token_major(s["akv"], Bd, T, kv_shape))
        outs[5].append(_token_major(s["idxk"], Bd, T, (D_IDX,)))
        outs[6].append(_token_major(s["bkv"], Bd, T, kv_shape))
        outs[7].append(_token_major(s["logf"], Bd, T, (N_HEADS,)))

    return (h_p.reshape(B, S, D), h_s.reshape(Bd, T, D)) + tuple(jnp.stack(o) for o in outs)
```
